```python
import jax
import jax.numpy as jnp
from jax import lax
import numpy as np

D_MODEL = 1024
BATCH = 4
SEQ = 4096
DEPTH = 2
DEC_BATCH = 128
DEC_SEQ = 8
PAST_LEN = 2048
PAGE_SIZE = 128

EPS = 1e-6
DN_HEADS = 4
DN_HEAD_DIM = 128
DN_WIDTH = DN_HEADS * DN_HEAD_DIM
CONV_WIDTH = 4
DN_CHUNK = 64
SWA_GROUPS = ((128, 1), (512, 4), (2048, 16))
N_SWA_GROUPS = len(SWA_GROUPS)
SWA_HEADS = 4
SWA_HEAD_DIM = 64
SWA_WIDTH = N_SWA_GROUPS * SWA_HEADS * SWA_HEAD_DIM
SWA_OUT = SWA_HEADS * SWA_HEAD_DIM
SWA_BLOCK = 128
ROPE_THETA = 10000.0
POOL_WINDOWS = (2, 4, 8, 16)
N_POOL_GROUPS = len(POOL_WINDOWS)
POOL_GROUP = 384
POOL_WIDTH = N_POOL_GROUPS * POOL_GROUP
POOL_BUF = max(POOL_WINDOWS) - 1
N_BRANCH = 3
IN_SIZES = (3 * DN_WIDTH, DN_WIDTH, DN_HEADS, DN_HEADS, 3 * SWA_WIDTH, POOL_WIDTH, N_BRANCH * D_MODEL)
IN_COLS = sum(IN_SIZES)
PEER_KEYS = 128
PEER_EXPERTS = PEER_KEYS * PEER_KEYS
PEER_HEADS = 8
PEER_TOPK = 16
PEER_DKEY = 256
PEER_BLOCK = 128

kernel_name = "hybrid_deltanet_dilated_pool_peer_step"


def _rmsnorm(x, g):
    xf = x.astype(jnp.float32)
    y = xf * lax.rsqrt(jnp.mean(xf * xf, axis=-1, keepdims=True) + EPS)
    return (y * g.astype(jnp.float32)).astype(x.dtype)


def _l2norm(x):
    return x * lax.rsqrt(jnp.sum(x * x, axis=-1, keepdims=True) + EPS)


def _rope(x, pos):
    half = x.shape[-1] // 2
    inv = ROPE_THETA ** (-jnp.arange(half, dtype=jnp.float32) / half)
    ang = pos.astype(jnp.float32)[:, None] * inv[None, :]
    shape = (1, x.shape[1]) + (1,) * (x.ndim - 3) + (half,)
    cos = jnp.cos(ang).reshape(shape)
    sin = jnp.sin(ang).reshape(shape)
    xf = x.astype(jnp.float32)
    x1, x2 = xf[..., :half], xf[..., half:]
    return jnp.concatenate([x1 * cos - x2 * sin, x1 * sin + x2 * cos], axis=-1).astype(x.dtype)


def _split_cols(proj):
    parts, off = [], 0
    for size in IN_SIZES:
        parts.append(proj[..., off:off + size])
        off += size
    return parts


def _gated_delta_rule(q, k, v, g, beta, state0):
    B, T, H, dk = q.shape
    dv = v.shape[-1]
    C = min(DN_CHUNK, T)
    pad = (-T) % C
    N = (T + pad) // C

    def chunks(a):
        a = jnp.pad(a, ((0, 0), (0, pad)) + ((0, 0),) * (a.ndim - 2))
        a = a.reshape((B, N, C) + a.shape[2:])
        return jnp.moveaxis(a, 3, 1)

    qc, kc, vc, gc, bc = chunks(q), chunks(k), chunks(v), chunks(g), chunks(beta)
    gc = jnp.cumsum(gc, axis=-1)
    idx = jnp.arange(C)
    incl = idx[:, None] >= idx[None, :]
    strict = idx[:, None] > idx[None, :]
    decay = jnp.exp(jnp.where(incl, gc[..., :, None] - gc[..., None, :], -jnp.inf))
    kb = kc * bc[..., None]
    vb = vc * bc[..., None]
    lower = jnp.einsum('bhncd,bhnsd->bhncs', kb, kc) * jnp.where(strict, decay, 0.0)
    rhs = jnp.concatenate([vb, kb * jnp.exp(gc)[..., None]], axis=-1)
    sol = lax.linalg.triangular_solve(lower, rhs, left_side=True, lower=True, unit_diagonal=True)
    u_c, w_c = sol[..., :dv], sol[..., dv:]
    intra = jnp.einsum('bhncd,bhnsd->bhncs', qc, kc) * decay
    q_dec = qc * jnp.exp(gc)[..., None]
    k_dec = kc * jnp.exp(gc[..., -1:] - gc)[..., None]
    g_last = jnp.exp(gc[..., -1])

    def step(S, inp):
        intra_n, qd_n, u_n, w_n, kd_n, gl_n = inp
        v_new = u_n - jnp.einsum('bhcd,bhde->bhce', w_n, S)
        o_n = jnp.einsum('bhcd,bhde->bhce', qd_n, S) + jnp.einsum('bhcs,bhse->bhce', intra_n, v_new)
        S = S * gl_n[..., None, None] + jnp.einsum('bhcd,bhce->bhde', kd_n, v_new)
        return S, o_n

    xs = tuple(jnp.moveaxis(a, 2, 0) for a in (intra, q_dec, u_c, w_c, k_dec, g_last))
    S, o = lax.scan(step, state0, xs)
    o = jnp.transpose(o, (1, 0, 3, 2, 4)).reshape(B, N * C, H, dv)[:, :T]
    return o, S


def _deltanet(qkv, z, a, b, conv0, state0, conv_w, a_log, dt_bias, dn_norm):
    B, T, Cq = qkv.shape
    ext = jnp.concatenate([conv0.astype(qkv.dtype), qkv], axis=1)
    conv = lax.conv_general_dilated(ext, conv_w[:, None, :].astype(qkv.dtype), window_strides=(1,), padding='VALID',
                                    dimension_numbers=('NWC', 'WIO', 'NWC'), feature_group_count=Cq)
    conv_new = ext[:, ext.shape[1] - (CONV_WIDTH - 1):]
    act = jax.nn.silu(conv.astype(jnp.float32)).reshape(B, T, 3, DN_HEADS, DN_HEAD_DIM)
    q = _l2norm(act[:, :, 0]) * (DN_HEAD_DIM ** -0.5)
    k = _l2norm(act[:, :, 1])
    v = act[:, :, 2]
    beta = jax.nn.sigmoid(b.astype(jnp.float32))
    g = -jnp.exp(a_log.astype(jnp.float32)) * jax.nn.softplus(a.astype(jnp.float32) + dt_bias.astype(jnp.float32))
    o, s_new = _gated_delta_rule(q, k, v, g, beta, state0.astype(jnp.float32))
    zf = z.astype(jnp.float32).reshape(B, T, DN_HEADS, DN_HEAD_DIM)
    o = _rmsnorm(o, dn_norm) * jax.nn.silu(zf)
    return o.reshape(B, T, DN_WIDTH).astype(qkv.dtype), conv_new, s_new.astype(state0.dtype)


def _dilated_prompt(q, k, v, window, dil):
    B, T, H, dh = q.shape
    n_back = window // dil
    L = T // dil
    Bq = SWA_BLOCK
    n_prev = -(-n_back // Bq)
    Lp = -(-L // Bq) * Bq
    nb = Lp // Bq
    z = B * dil

    def phase(a):
        a = a.reshape(B, L, dil, H, dh).transpose(0, 2, 1, 3, 4).reshape(z, L, H, dh)
        return jnp.pad(a.astype(jnp.float32), ((0, 0), (0, Lp - L), (0, 0), (0, 0)))

    qs, ks, vs = phase(q), phase(k), phase(v)
    front = ((0, 0), (n_prev * Bq, 0), (0, 0), (0, 0))
    kp, vp = jnp.pad(ks, front), jnp.pad(vs, front)

    def band(a):
        return jnp.concatenate([a[:, p * Bq:p * Bq + Lp].reshape(z, nb, Bq, H, dh) for p in range(n_prev + 1)], axis=2)

    kband, vband = band(kp), band(vp)
    qbl = qs.reshape(z, nb, Bq, H, dh)
    s = jnp.einsum('znqhd,znkhd->znhqk', qbl, kband) * (dh ** -0.5)
    qi = jnp.arange(Bq)
    kj = jnp.arange((n_prev + 1) * Bq)
    rel = n_prev * Bq + qi[:, None] - kj[None, :]
    qpos = jnp.arange(nb)[:, None, None] * Bq + qi[None, :, None]
    valid = (rel >= 0) & (rel <= n_back) & (qpos - rel >= 0)
    s = jnp.where(valid[None, :, None], s, -jnp.inf)
    m = jnp.max(s, axis=-1)
    p = jnp.exp(s - m[..., None])
    l = jnp.sum(p, axis=-1)
    o = jnp.einsum('znhqk,znkhd->znqhd', p, vband).reshape(z, Lp, H, dh)
    m = jnp.transpose(m, (0, 1, 3, 2)).reshape(z, Lp, H)
    l = jnp.transpose(l, (0, 1, 3, 2)).reshape(z, Lp, H)

    def unphase(a):
        a = a[:, :L].reshape((B, dil, L) + a.shape[2:])
        return jnp.moveaxis(a, 1, 2).reshape((B, T) + a.shape[3:])

    return unphase(o), unphase(m), unphase(l)


def _dilated_sample(q, kcat, vcat, n_buf, window, dil):
    B, T, H, dh = q.shape
    n_back = window // dil
    idx = n_buf + jnp.arange(T)[:, None] - dil * jnp.arange(n_back + 1)[None, :]
    valid = idx >= 0
    idx = jnp.maximum(idx, 0)
    kg = kcat[:, idx].astype(jnp.float32)
    vg = vcat[:, idx].astype(jnp.float32)
    s = jnp.einsum('bthd,btjhd->bthj', q.astype(jnp.float32), kg) * (dh ** -0.5)
    s = jnp.where(valid[None, :, None, :], s, -jnp.inf)
    m = jnp.max(s, axis=-1)
    p = jnp.exp(s - m[..., None])
    l = jnp.sum(p, axis=-1)
    o = jnp.einsum('bthj,btjhd->bthd', p, vg)
    return o, m, l


def _combine_groups(parts):
    m_all = jnp.stack([pt[1] for pt in parts], axis=0)
    M = jnp.max(m_all, axis=0)
    num = sum(jnp.exp(pt[1] - M)[..., None] * pt[0] for pt in parts)
    den = sum(jnp.exp(pt[1] - M) * pt[2] for pt in parts)
    return num / den[..., None]


def _dilated_branch(qkv_b, pos0, prompt, wins0):
    B, T, _ = qkv_b.shape
    qkv = qkv_b.reshape(B, T, 3, N_SWA_GROUPS, SWA_HEADS, SWA_HEAD_DIM)
    pos = pos0 + jnp.arange(T)
    q = _rope(qkv[:, :, 0], pos)
    k = _rope(qkv[:, :, 1], pos)
    v = qkv[:, :, 2]
    parts, new_wins = [], []
    for gi, (window, dil) in enumerate(SWA_GROUPS):
        qg, kg, vg = q[:, :, gi], k[:, :, gi], v[:, :, gi]
        kv_new = jnp.stack([kg, vg], axis=2)
        if prompt:
            parts.append(_dilated_prompt(qg, kg, vg, window, dil))
            kv_all = kv_new
        else:
            buf = wins0[gi]
            kv_all = jnp.concatenate([buf.astype(kv_new.dtype), kv_new], axis=1)
            parts.append(_dilated_sample(qg, kv_all[:, :, 0], kv_all[:, :, 1], buf.shape[1], window, dil))
        keep = min(window, kv_all.shape[1])
        new_wins.append(kv_all[:, kv_all.shape[1] - keep:])
    o = _combine_groups(parts)
    return o.reshape(B, T, SWA_OUT).astype(qkv_b.dtype), new_wins


def _pool_branch(u, prefix, pos0, w_pool, pool_scale):
    B, T, _ = u.shape
    ext = jnp.concatenate([prefix.astype(u.dtype), u], axis=1)
    csum = jnp.pad(jnp.cumsum(ext.astype(jnp.float32), axis=1), ((0, 0), (1, 0), (0, 0)))
    end = csum[:, POOL_BUF + 1:POOL_BUF + 1 + T]
    pos = (pos0 + jnp.arange(T)).astype(jnp.float32)
    uf = u.astype(jnp.float32)
    diffs = []
    for gi, w in enumerate(POOL_WINDOWS):
        lo, hi = gi * POOL_GROUP, (gi + 1) * POOL_GROUP
        start = csum[:, POOL_BUF + 1 - w:POOL_BUF + 1 - w + T, lo:hi]
        count = jnp.minimum(float(w), pos + 1.0)[None, :, None]
        diffs.append((end[..., lo:hi] - start) / count - uf[..., lo:hi])
    d = jnp.stack(diffs, axis=2)
    y = jnp.einsum('btgc,gce->btge', d, w_pool.astype(jnp.float32)).reshape(B, T, POOL_WIDTH)
    return (y * pool_scale.astype(jnp.float32)).astype(u.dtype), ext[:, ext.shape[1] - POOL_BUF:]


def _peer(h, w_query, subkeys, w_up, w_down):
    B, T, D = h.shape
    n = B * T
    xt = h.reshape(n, D)
    q = (xt @ w_query).astype(jnp.float32).reshape(n, PEER_HEADS, 2, PEER_DKEY // 2)
    s = jnp.einsum('nhpd,pkd->nhpk', q, subkeys.astype(jnp.float32))
    s1, i1 = lax.top_k(s[:, :, 0], PEER_TOPK)
    s2, i2 = lax.top_k(s[:, :, 1], PEER_TOPK)
    cand = (s1[..., :, None] + s2[..., None, :]).reshape(n, PEER_HEADS, PEER_TOPK * PEER_TOPK)
    best, ic = lax.top_k(cand, PEER_TOPK)
    expert = (jnp.take_along_axis(i1, ic // PEER_TOPK, axis=-1) * PEER_KEYS
              + jnp.take_along_axis(i2, ic % PEER_TOPK, axis=-1))
    gate = jax.nn.softmax(best, axis=-1)
    nb = -(-n // PEER_BLOCK)
    pad = nb * PEER_BLOCK - n
    xb = jnp.pad(xt, ((0, pad), (0, 0))).reshape(nb, PEER_BLOCK, D)
    eb = jnp.pad(expert, ((0, pad), (0, 0), (0, 0))).reshape(nb, PEER_BLOCK, PEER_HEADS, PEER_TOPK)
    gb = jnp.pad(gate, ((0, pad), (0, 0), (0, 0))).reshape(nb, PEER_BLOCK, PEER_HEADS, PEER_TOPK)

    def block(args):
        xk, ek, gk = args
        up = jnp.take(w_up, ek, axis=0).astype(jnp.float32)
        act = jax.nn.gelu(jnp.einsum('nd,nhkd->nhk', xk.astype(jnp.float32), up), approximate=False) * gk
        down = jnp.take(w_down, ek, axis=0).astype(jnp.float32)
        return jnp.einsum('nhk,nhkd->nd', act, down)

    y = lax.map(block, (xb, eb, gb)).reshape(nb * PEER_BLOCK, D)[:n]
    return y.reshape(B, T, D).astype(h.dtype)


def _layer(x, l, pos0, prompt, delta0, conv0, wins0, pool0, weights):
    (norm_mix, w_in, conv_w, a_log, dt_bias, dn_norm, w_pool, pool_scale, w_branch, w_out,
     norm_ffn, peer_query, peer_subkeys, peer_up, peer_down) = weights
    B, T, _ = x.shape
    h = _rmsnorm(x, norm_mix[l])
    qkv_a, z_a, a_a, b_a, qkv_b, u_c, gate_logits = _split_cols(h @ w_in[l])
    o_a, conv_new, delta_new = _deltanet(qkv_a, z_a, a_a, b_a, conv0, delta0, conv_w[l], a_log[l], dt_bias[l], dn_norm[l])
    o_b, wins_new = _dilated_branch(qkv_b, pos0, prompt, wins0)
    o_c, pool_new = _pool_branch(u_c, pool0, pos0, w_pool[l], pool_scale[l])
    gates = jax.nn.sigmoid(gate_logits.astype(jnp.float32)).astype(x.dtype).reshape(B, T, N_BRANCH, D_MODEL)
    wb = w_branch[l]
    merged = (gates[:, :, 0] * (o_a @ wb[:DN_WIDTH])
              + gates[:, :, 1] * (o_b @ wb[DN_WIDTH:DN_WIDTH + SWA_OUT])
              + gates[:, :, 2] * (o_c @ wb[DN_WIDTH + SWA_OUT:]))
    x = x + merged @ w_out[l]
    x = x + _peer(_rmsnorm(x, norm_ffn[l]), peer_query[l], peer_subkeys[l], peer_up[l], peer_down[l])
    return x, delta_new, conv_new, wins_new, pool_new


def _trunk(x, pos0, prompt, deltas, convs, wins, pools, weights, norm_final):
    new_delta, new_conv, new_pool = [], [], []
    new_wins = [[] for _ in SWA_GROUPS]
    for l in range(DEPTH):
        wins_l = None if prompt else [w[l] for w in wins]
        x, d, c, ws, p = _layer(x, l, pos0, prompt, deltas[l], convs[l], wins_l, pools[l], weights)
        new_delta.append(d)
        new_conv.append(c)
        new_pool.append(p)
        for gi in range(N_SWA_GROUPS):
            new_wins[gi].append(ws[gi])
    y = _rmsnorm(x, norm_final)
    return y, jnp.stack(new_delta), jnp.stack(new_conv), [jnp.stack(w) for w in new_wins], jnp.stack(new_pool)


def setup_inputs(seed: int = 0) -> dict:
    key = jax.random.key(seed)
    ks = jax.random.split(key, 32)
    f32 = jnp.float32

    def nrm(k, shape, scale):
        return jax.random.normal(k, shape, f32) * scale

    win_lens = [min(w, PAST_LEN) for w, _ in SWA_GROUPS]
    dt = jnp.exp(jax.random.uniform(ks[12], (DEPTH, DN_HEADS), f32, float(np.log(1e-3)), float(np.log(1e-1))))
    return {
        'x_prompt': nrm(ks[0], (BATCH, SEQ, D_MODEL), 1.0),
        'x_sample': nrm(ks[1], (DEC_BATCH, DEC_SEQ, D_MODEL), 1.0),
        'state_delta': nrm(ks[2], (DEPTH, DEC_BATCH, DN_HEADS, DN_HEAD_DIM, DN_HEAD_DIM), 0.05),
        'state_conv': nrm(ks[3], (DEPTH, DEC_BATCH, CONV_WIDTH - 1, 3 * DN_WIDTH), 1.0),
        'cache_win0': nrm(ks[4], (DEPTH, DEC_BATCH, win_lens[0], 2, SWA_HEADS, SWA_HEAD_DIM), 1.0),
        'cache_win1': nrm(ks[5], (DEPTH, DEC_BATCH, win_lens[1], 2, SWA_HEADS, SWA_HEAD_DIM), 1.0),
        'cache_win2': nrm(ks[6], (DEPTH, DEC_BATCH, win_lens[2], 2, SWA_HEADS, SWA_HEAD_DIM), 1.0),
        'state_pool': nrm(ks[7], (DEPTH, DEC_BATCH, POOL_BUF, POOL_WIDTH), 1.0),
        'norm_mix': 1.0 + nrm(ks[8], (DEPTH, D_MODEL), 0.02),
        'w_in': nrm(ks[9], (DEPTH, D_MODEL, IN_COLS), D_MODEL ** -0.5),
        'conv_w': nrm(ks[10], (DEPTH, CONV_WIDTH, 3 * DN_WIDTH), CONV_WIDTH ** -0.5),
        'a_log': jnp.log(jax.random.uniform(ks[11], (DEPTH, DN_HEADS), f32, 1.0, 16.0)),
        'dt_bias': dt + jnp.log(-jnp.expm1(-dt)),
        'dn_norm': 1.0 + nrm(ks[13], (DEPTH, DN_HEAD_DIM), 0.02),
        'w_pool': nrm(ks[14], (DEPTH, N_POOL_GROUPS, POOL_GROUP, POOL_GROUP), POOL_GROUP ** -0.5),
        'pool_scale': 1.0 + nrm(ks[15], (DEPTH, POOL_WIDTH), 0.02),
        'w_branch': jnp.concatenate([nrm(ks[16], (DEPTH, DN_WIDTH, D_MODEL), DN_WIDTH ** -0.5),
                                     nrm(ks[17], (DEPTH, SWA_OUT, D_MODEL), SWA_OUT ** -0.5),
                                     nrm(ks[18], (DEPTH, POOL_WIDTH, D_MODEL), POOL_WIDTH ** -0.5)], axis=1),
        'w_out': nrm(ks[19], (DEPTH, D_MODEL, D_MODEL), D_MODEL ** -0.5),
        'norm_ffn': 1.0 + nrm(ks[20], (DEPTH, D_MODEL), 0.02),
        'peer_query': nrm(ks[21], (DEPTH, D_MODEL, PEER_HEADS * PEER_DKEY), D_MODEL ** -0.5),
        'peer_subkeys': nrm(ks[22], (DEPTH, 2, PEER_KEYS, PEER_DKEY // 2), (PEER_DKEY // 2) ** -0.5),
        'peer_up': nrm(ks[23], (DEPTH, PEER_EXPERTS, D_MODEL), D_MODEL ** -0.5),
        'peer_down': nrm(ks[24], (DEPTH, PEER_EXPERTS, D_MODEL), PEER_HEADS ** -0.5),
        'norm_final': 1.0 + nrm(ks[25], (D_MODEL,), 0.02),
    }


def reference(x_prompt, x_sample, state_delta, state_conv, cache_win0, cache_win1, cache_win2, state_pool,
              norm_mix, w_in, conv_w, a_log, dt_bias, dn_norm, w_pool, pool_scale, w_branch, w_out,
              norm_ffn, peer_query, peer_subkeys, peer_up, peer_down, norm_final):
    weights = (norm_mix, w_in, conv_w, a_log, dt_bias, dn_norm, w_pool, pool_scale, w_branch, w_out,
               norm_ffn, peer_query, peer_subkeys, peer_up, peer_down)
    Bp = x_prompt.shape[0]
    zd = jnp.zeros((Bp, DN_HEADS, DN_HEAD_DIM, DN_HEAD_DIM), state_delta.dtype)
    zc = jnp.zeros((Bp, CONV_WIDTH - 1, 3 * DN_WIDTH), x_prompt.dtype)
    zp = jnp.zeros((Bp, POOL_BUF, POOL_WIDTH), x_prompt.dtype)
    y_prompt, delta_p, conv_p, win_p, pool_p = _trunk(
        x_prompt, 0, True, [zd] * DEPTH, [zc] * DEPTH, None, [zp] * DEPTH, weights, norm_final)
    y_sample, delta_s, conv_s, win_s, pool_s = _trunk(
        x_sample, PAST_LEN, False, state_delta, state_conv, (cache_win0, cache_win1, cache_win2), state_pool,
        weights, norm_final)
    return (y_prompt, y_sample, delta_p, delta_s, conv_p, conv_s,
            win_p[0], win_s[0], win_p[1], win_s[1], win_p[2], win_s[2], pool_p, pool_s)
```

```python
import functools

import jax
import jax.numpy as jnp
import numpy as np
from jax import lax
from jax.experimental import pallas as pl
from jax.experimental.pallas import tpu as pltpu

F32 = jnp.float32
BF16 = jnp.bfloat16
HI = lax.Precision.HIGHEST
EPS = 1e-6
NEG_INF = float("-inf")

D_MODEL = 1024
DN_HEADS = 4
DN_DIM = 128
DN_CHUNK = 64
CONV_W = 4
SWA_GROUPS = ((128, 1), (512, 4), (2048, 16))
SWA_HEADS = 4
SWA_DIM = 64
SWA_OUT = SWA_HEADS * SWA_DIM
SWA_BLOCK = 128
ROPE_THETA = 10000.0
POOL_WINDOWS = (2, 4, 8, 16)
POOL_GROUP = 384
POOL_WIDTH = 1536
POOL_BUF = 15
PEER_KEYS = 128
PEER_HEADS = 8
PEER_TOPK = 16
PEER_SEL = PEER_HEADS * PEER_TOPK
PAST_LEN = 2048

COL_QKV_A = 0
COL_U = 1536
COL_GATE = 3072
COL_Z = 6144
COL_QKV_B = 6656
COL_AB = 8960
P_COLS = 9216
VMEM_LIMIT = 56 * 1024 * 1024


def _cparams(sem):
    return pltpu.CompilerParams(dimension_semantics=sem, vmem_limit_bytes=VMEM_LIMIT)


def _sigmoid(x):
    return 1.0 / (1.0 + jnp.exp(-x))


def _silu(x):
    return x * _sigmoid(x)


def _inproj_kernel(x_ref, g_ref, w_ref, o_ref, h_scr):
    @pl.when(pl.program_id(1) == 0)
    def _():
        x = x_ref[...]
        y = x * lax.rsqrt(jnp.mean(x * x, axis=-1, keepdims=True) + EPS)
        h_scr[...] = (y * g_ref[...]).astype(BF16)

    o_ref[...] = jnp.dot(h_scr[...], w_ref[...], preferred_element_type=F32)


def _inproj(x, g, w_bf, tm=512, tn=1024):
    n, d = x.shape
    nc = w_bf.shape[1]
    return pl.pallas_call(
        _inproj_kernel,
        grid=(n // tm, nc // tn),
        in_specs=[pl.BlockSpec((tm, d), lambda i, j: (i, 0)),
                  pl.BlockSpec((1, d), lambda i, j: (0, 0)),
                  pl.BlockSpec((d, tn), lambda i, j: (0, j))],
        out_specs=pl.BlockSpec((tm, tn), lambda i, j: (i, j)),
        out_shape=jax.ShapeDtypeStruct((n, nc), F32),
        scratch_shapes=[pltpu.VMEM((tm, d), BF16)],
        compiler_params=_cparams(("parallel", "arbitrary")),
        name="inproj",
    )(x, g.reshape(1, d), w_bf)


def _pack_w_in(w):
    d = w.shape[0]
    parts = [w[:, 0:1536], w[:, 4360:5896], w[:, 5896:8968], w[:, 1536:2048], w[:, 2056:4360], w[:, 2048:2056],
             jnp.zeros((d, P_COLS - 8968), w.dtype)]
    return jnp.concatenate(parts, axis=1).astype(BF16)


def _dn_kernel(alog_ref, dtb_ref, q_ref, k_ref, v_ref, z_ref, ab_ref, abT_ref, c0q_ref, c0k_ref, c0v_ref,
               cwq_ref, cwk_ref, cwv_ref, s0_ref, dnn_ref, o_ref, sout_ref, S_scr, ext_scr, *, C, Tc):
    h = pl.program_id(1)
    t = pl.program_id(2)

    @pl.when(t == 0)
    def _():
        S_scr[...] = s0_ref[0, 0]
        ext_scr[0, 0:8, :] = c0q_ref[0]
        ext_scr[1, 0:8, :] = c0k_ref[0]
        ext_scr[2, 0:8, :] = c0v_ref[0]

    def conv_act(i, x_ref, cw_ref):
        x = x_ref[...]
        ext_scr[i, 8:8 + Tc, :] = x
        w = cw_ref[...]
        acc = (ext_scr[i, 5:5 + Tc, :] * w[0:1] + ext_scr[i, 6:6 + Tc, :] * w[1:2]
               + ext_scr[i, 7:7 + Tc, :] * w[2:3] + x * w[3:4])
        ext_scr[i, 0:8, :] = ext_scr[i, Tc:Tc + 8, :]
        return _silu(acc)

    def l2n(x):
        return x * lax.rsqrt(jnp.sum(x * x, axis=-1, keepdims=True) + EPS)

    q_all = l2n(conv_act(0, q_ref, cwq_ref)) * (DN_DIM ** -0.5)
    k_all = l2n(conv_act(1, k_ref, cwk_ref))
    v_all = conv_act(2, v_ref, cwv_ref)

    ab = ab_ref[...]
    lane = lax.broadcasted_iota(jnp.int32, ab.shape, 1)
    a_col = jnp.sum(jnp.where(lane == h, ab, 0.0), axis=1, keepdims=True)
    b_col = jnp.sum(jnp.where(lane == h + DN_HEADS, ab, 0.0), axis=1, keepdims=True)
    abT = abT_ref[0]
    sub = lax.broadcasted_iota(jnp.int32, abT.shape, 0)
    a_row = jnp.sum(jnp.where(sub == h, abT, 0.0), axis=0, keepdims=True)

    neg_a = -jnp.exp(jnp.full((1, 1), alog_ref[h], F32))
    dtb = dtb_ref[h]

    def softplus(x):
        return jnp.maximum(x, 0.0) + jnp.log1p(jnp.exp(-jnp.abs(x)))

    g_col_all = neg_a * softplus(a_col + dtb)
    g_row_all = neg_a * softplus(a_row + dtb)
    beta_all = _sigmoid(b_col)
    zs = _silu(z_ref[...])
    dnn = dnn_ref[...]

    ii = lax.broadcasted_iota(jnp.int32, (C, C), 0)
    jj = lax.broadcasted_iota(jnp.int32, (C, C), 1)
    incl = ii >= jj
    strict = ii > jj
    eye = jnp.where(ii == jj, 1.0, 0.0).astype(F32)

    def mm(a, b):
        return jnp.dot(a, b, precision=HI, preferred_element_type=F32)

    def mm_nt(a, b):
        return lax.dot_general(a, b, (((1,), (1,)), ((), ())), precision=HI, preferred_element_type=F32)

    def mm_tn(a, b):
        return lax.dot_general(a, b, (((0,), (0,)), ((), ())), precision=HI, preferred_element_type=F32)

    S = S_scr[...]
    for c in range(Tc // C):
        sl = slice(c * C, (c + 1) * C)
        qc, kc, vc = q_all[sl], k_all[sl], v_all[sl]
        g_col, g_row, beta = g_col_all[sl], g_row_all[:, sl], beta_all[sl]
        gc_col = jnp.sum(jnp.where(incl, g_row, 0.0), axis=1, keepdims=True)
        gc_row = jnp.sum(jnp.where(ii <= jj, g_col, 0.0), axis=0, keepdims=True)
        gc_last = jnp.sum(g_row, axis=1, keepdims=True)
        decay = jnp.where(incl, jnp.exp(jnp.where(incl, gc_col - gc_row, 0.0)), 0.0)
        kb = kc * beta
        vb = vc * beta
        egc = jnp.exp(gc_col)
        lower = mm_nt(kb, kc) * jnp.where(strict, decay, 0.0)
        pw = -lower
        inv = eye + pw
        m = 2
        while m < C:
            pw = mm(pw, pw)
            inv = inv + mm(inv, pw)
            m *= 2
        u = mm(inv, vb)
        w = mm(inv, kb * egc)
        intra = mm_nt(qc, kc) * decay
        q_dec = qc * egc
        k_dec = kc * jnp.exp(gc_last - gc_col)
        v_new = u - mm(w, S)
        o = mm(q_dec, S) + mm(intra, v_new)
        S = S * jnp.exp(gc_last) + mm_tn(k_dec, v_new)
        o = o * lax.rsqrt(jnp.mean(o * o, axis=-1, keepdims=True) + EPS) * dnn
        o_ref[sl, :] = o * zs[sl]
    S_scr[...] = S

    @pl.when(t == pl.num_programs(2) - 1)
    def _():
        sout_ref[0, 0] = S


def _deltanet(P, abT, conv0p, cw8, state0, a_log, dt_bias, dn_norm, row0, B, T, Tc, C):
    n = P.shape[0]
    nt = T // Tc
    rb0 = row0 // Tc
    H = DN_HEADS

    def rows(col0):
        return pl.BlockSpec((Tc, 128), lambda b, h, t, c=col0: (rb0 + b * nt + t, c + h))

    def c0(col0):
        return pl.BlockSpec((1, 8, 128), lambda b, h, t, c=col0: (b, 0, c + h))

    def cw(col0):
        return pl.BlockSpec((8, 128), lambda b, h, t, c=col0: (0, c + h))

    smem = pl.BlockSpec(memory_space=pltpu.SMEM)
    o, s_new = pl.pallas_call(
        functools.partial(_dn_kernel, C=C, Tc=Tc),
        grid=(B, H, nt),
        in_specs=[smem, smem, rows(0), rows(4), rows(8), rows(COL_Z // 128),
                  pl.BlockSpec((Tc, 128), lambda b, h, t: (rb0 + b * nt + t, COL_AB // 128)),
                  pl.BlockSpec((1, 8, Tc), lambda b, h, t: (b, 0, t)),
                  c0(0), c0(4), c0(8), cw(0), cw(4), cw(8),
                  pl.BlockSpec((1, 1, 128, 128), lambda b, h, t: (b, h, 0, 0)),
                  pl.BlockSpec((1, 128), lambda b, h, t: (0, 0))],
        out_specs=[pl.BlockSpec((Tc, 128), lambda b, h, t: (b * nt + t, h)),
                   pl.BlockSpec((1, 1, 128, 128), lambda b, h, t: (b, h, 0, 0))],
        out_shape=[jax.ShapeDtypeStruct((B * T, H * 128), F32),
                   jax.ShapeDtypeStruct((B, H, 128, 128), F32)],
        scratch_shapes=[pltpu.VMEM((128, 128), F32), pltpu.VMEM((3, Tc + 8, 128), F32)],
        compiler_params=_cparams(("parallel", "parallel", "arbitrary")),
        name="deltanet",
    )(a_log, dt_bias, P, P, P, P, P, abT, conv0p, conv0p, conv0p, cw8, cw8, cw8, state0, dn_norm.reshape(1, 128))
    return o, s_new


def _rope_tables(pos):
    half = SWA_DIM // 2
    inv = ROPE_THETA ** (-jnp.arange(half, dtype=F32) / half)
    ang = pos.astype(F32)[:, None] * inv[None, :]
    cos = jnp.tile(jnp.cos(ang), (1, 2 * SWA_HEADS))
    sin = jnp.tile(jnp.sin(ang), (1, 2 * SWA_HEADS))
    return cos, sin


def _rope(x, cos, sin):
    lane = lax.broadcasted_iota(jnp.int32, x.shape, 1)
    first = (lane % SWA_DIM) < (SWA_DIM // 2)
    w = x.shape[1]
    xr = jnp.where(first, -pltpu.roll(x, w - SWA_DIM // 2, 1), pltpu.roll(x, SWA_DIM // 2, 1))
    return x * cos + xr * sin


def _mm_nt(a, b):
    return lax.dot_general(a, b, (((1,), (1,)), ((), ())), precision=HI, preferred_element_type=F32)


def _mm(a, b):
    return jnp.dot(a, b, precision=HI, preferred_element_type=F32)


def _attn_prompt_kernel(q_ref, kc_ref, kp_ref, vc_ref, vp_ref, cq_ref, sq_ref, cp_ref, sp_ref,
                        o_ref, m_ref, l_ref, kr_ref, *, n_back):
    i = pl.program_id(2)
    Bq = SWA_BLOCK
    q = _rope(q_ref[...], cq_ref[...], sq_ref[...])
    k_cur = _rope(kc_ref[...], cq_ref[...], sq_ref[...])
    k_prev = _rope(kp_ref[...], cp_ref[...], sp_ref[...])
    kr_ref[...] = k_cur
    kcat = jnp.concatenate([k_prev, k_cur], axis=0)
    vcat = jnp.concatenate([vp_ref[...], vc_ref[...]], axis=0)
    qi = lax.broadcasted_iota(jnp.int32, (Bq, 2 * Bq), 0)
    kj = lax.broadcasted_iota(jnp.int32, (Bq, 2 * Bq), 1)
    rel = Bq + qi - kj
    valid = (rel >= 0) & (rel <= n_back) & (i * Bq + qi - rel >= 0)
    head = lax.broadcasted_iota(jnp.int32, (Bq, SWA_OUT), 1) // SWA_DIM
    o = jnp.zeros((Bq, SWA_OUT), F32)
    m_full = jnp.zeros((Bq, SWA_OUT), F32)
    l_full = jnp.zeros((Bq, SWA_OUT), F32)
    for h in range(SWA_HEADS):
        hm = head == h
        s = _mm_nt(jnp.where(hm, q, 0.0), kcat) * (SWA_DIM ** -0.5)
        s = jnp.where(valid, s, NEG_INF)
        m = jnp.max(s, axis=1, keepdims=True)
        p = jnp.exp(s - m)
        l = jnp.sum(p, axis=1, keepdims=True)
        o = jnp.where(hm, _mm(p, vcat), o)
        m_full = jnp.where(hm, m, m_full)
        l_full = jnp.where(hm, l, l_full)
    o_ref[...] = o
    m_ref[...] = m_full
    l_ref[...] = l_full


def _attn_prompt(P, cos, sin, gi, B, T):
    window, dil = SWA_GROUPS[gi]
    n_back = window // dil
    assert n_back <= SWA_BLOCK
    n = P.shape[0]
    L = T // dil
    nb = L // SWA_BLOCK
    Pd = P.reshape(n // dil, dil * P_COLS)
    cw = P_COLS // SWA_OUT
    cd = cos.reshape(L, dil * SWA_OUT)
    sd = sin.reshape(L, dil * SWA_OUT)
    blk = (SWA_BLOCK, SWA_OUT)

    def pcol(which):
        return COL_QKV_B // SWA_OUT + which * 3 + gi

    def cur(which):
        return pl.BlockSpec(blk, lambda b, p, i, c=pcol(which): (b * nb + i, p * cw + c))

    def prev(which):
        return pl.BlockSpec(blk, lambda b, p, i, c=pcol(which): (b * nb + jnp.maximum(i - 1, 0), p * cw + c))

    tcur = pl.BlockSpec(blk, lambda b, p, i: (i, p))
    tprev = pl.BlockSpec(blk, lambda b, p, i: (jnp.maximum(i - 1, 0), p))
    ospec = pl.BlockSpec(blk, lambda b, p, i: (b * nb + i, p))
    oshape = jax.ShapeDtypeStruct((B * L, dil * SWA_OUT), F32)
    outs = pl.pallas_call(
        functools.partial(_attn_prompt_kernel, n_back=n_back),
        grid=(B, dil, nb),
        in_specs=[cur(0), cur(1), prev(1), cur(2), prev(2), tcur, tcur, tprev, tprev],
        out_specs=[ospec] * 4,
        out_shape=[oshape] * 4,
        compiler_params=_cparams(("parallel", "parallel", "arbitrary")),
        name=f"attn_prompt{gi}",
    )(Pd, Pd, Pd, Pd, Pd, cd, sd, cd, sd)
    return [a.reshape(B * T, SWA_OUT) for a in outs]


def _attn_sample_kernel(q_ref, k_ref, v_ref, cos_ref, sin_ref, buf_ref, o_ref, m_ref, l_ref, kr_ref, *, dil, nph):
    T = q_ref.shape[0]
    nbuf = buf_ref.shape[1]
    q = _rope(q_ref[...], cos_ref[...], sin_ref[...])
    k_new = _rope(k_ref[...], cos_ref[...], sin_ref[...])
    v_new = v_ref[...]
    kr_ref[...] = k_new
    head = lax.broadcasted_iota(jnp.int32, (T, SWA_OUT), 1) // SWA_DIM
    tq = lax.broadcasted_iota(jnp.int32, (T, nbuf), 0)
    mk = lax.broadcasted_iota(jnp.int32, (T, nbuf), 1)
    tq2 = lax.broadcasted_iota(jnp.int32, (T, T), 0)
    tk2 = lax.broadcasted_iota(jnp.int32, (T, T), 1)
    valid_new = ((tq2 % dil) == (tk2 % dil)) & (tk2 <= tq2)
    valid_buf = [((tq % dil) == p) & (mk >= (tq - p) // dil) for p in range(nph)]
    o = jnp.zeros((T, SWA_OUT), F32)
    m_full = jnp.zeros((T, SWA_OUT), F32)
    l_full = jnp.zeros((T, SWA_OUT), F32)
    for h in range(SWA_HEADS):
        hm = head == h
        qm = jnp.where(hm, q, 0.0)
        s_new = jnp.where(valid_new, _mm_nt(qm, k_new) * (SWA_DIM ** -0.5), NEG_INF)
        m = jnp.max(s_new, axis=1, keepdims=True)
        s_buf = []
        for p in range(nph):
            kb = buf_ref[0, :, p * 2 * SWA_OUT:p * 2 * SWA_OUT + SWA_OUT]
            s = jnp.where(valid_buf[p], _mm_nt(qm, kb) * (SWA_DIM ** -0.5), NEG_INF)
            s_buf.append(s)
            m = jnp.maximum(m, jnp.max(s, axis=1, keepdims=True))
        pn = jnp.exp(s_new - m)
        l = jnp.sum(pn, axis=1, keepdims=True)
        acc = _mm(pn, v_new)
        for p in range(nph):
            vb = buf_ref[0, :, p * 2 * SWA_OUT + SWA_OUT:(p + 1) * 2 * SWA_OUT]
            pp = jnp.exp(s_buf[p] - m)
            l = l + jnp.sum(pp, axis=1, keepdims=True)
            acc = acc + _mm(pp, vb)
        o = jnp.where(hm, acc, o)
        m_full = jnp.where(hm, m, m_full)
        l_full = jnp.where(hm, l, l_full)
    o_ref[...] = o
    m_ref[...] = m_full
    l_ref[...] = l_full


def _attn_sample(P, cos, sin, cache, gi, row0, B, T):
    window, dil = SWA_GROUPS[gi]
    W = cache.shape[1]
    assert W == window and W % dil == 0 and T <= 8 and row0 % T == 0
    nph = min(dil, T)
    cv = cache.reshape(B, W // dil, dil * 2 * SWA_OUT)
    rb0 = row0 // T
    blk = (T, SWA_OUT)

    def rows(which):
        return pl.BlockSpec(blk, lambda b, c=COL_QKV_B // SWA_OUT + which * 3 + gi: (rb0 + b, c))

    tab = pl.BlockSpec(blk, lambda b: (0, 0))
    ospec = pl.BlockSpec(blk, lambda b: (b, 0))
    oshape = jax.ShapeDtypeStruct((B * T, SWA_OUT), F32)
    return pl.pallas_call(
        functools.partial(_attn_sample_kernel, dil=dil, nph=nph),
        grid=(B,),
        in_specs=[rows(0), rows(1), rows(2), tab, tab,
                  pl.BlockSpec((1, W // dil, nph * 2 * SWA_OUT), lambda b: (b, 0, 0))],
        out_specs=[ospec] * 4,
        out_shape=[oshape] * 4,
        compiler_params=_cparams(("parallel",)),
        name=f"attn_sample{gi}",
    )(P, P, P, cos, sin, cv)


def _pool_kernel(u_ref, pre_ref, w_ref, sc_ref, o_ref, ext_scr, *, Tc, pos0):
    t = pl.program_id(1)
    nb = POOL_BUF + 1

    @pl.when(t == 0)
    def _():
        ext_scr[0:nb, :] = pre_ref[0]

    u = u_ref[...]
    ext_scr[nb:nb + Tc, :] = u
    pos = (pos0 + t * Tc + lax.broadcasted_iota(jnp.int32, (Tc, 1), 0)).astype(F32)
    for gi, w in enumerate(POOL_WINDOWS):
        lo, hi = gi * POOL_GROUP, (gi + 1) * POOL_GROUP
        ug = u[:, lo:hi]
        acc = ug
        for i in range(1, w):
            acc = acc + ext_scr[nb - i:nb - i + Tc, lo:hi]
        d = acc / jnp.minimum(float(w), pos + 1.0) - ug
        y = jnp.dot(d.astype(BF16), w_ref[gi], preferred_element_type=F32)
        o_ref[:, lo:hi] = y * sc_ref[:, lo:hi]
    ext_scr[0:nb, :] = ext_scr[Tc:Tc + nb, :]


def _pool(P, prefix16, w_pool_bf, pool_scale, row0, B, T, Tc, pos0):
    nt = T // Tc
    rb0 = row0 // Tc
    return pl.pallas_call(
        functools.partial(_pool_kernel, Tc=Tc, pos0=pos0),
        grid=(B, nt),
        in_specs=[pl.BlockSpec((Tc, POOL_WIDTH), lambda b, t: (rb0 + b * nt + t, COL_U // POOL_WIDTH)),
                  pl.BlockSpec((1, POOL_BUF + 1, POOL_WIDTH), lambda b, t: (b, 0, 0)),
                  pl.BlockSpec((len(POOL_WINDOWS), POOL_GROUP, POOL_GROUP), lambda b, t: (0, 0, 0)),
                  pl.BlockSpec((1, POOL_WIDTH), lambda b, t: (0, 0))],
        out_specs=pl.BlockSpec((Tc, POOL_WIDTH), lambda b, t: (b * nt + t, 0)),
        out_shape=jax.ShapeDtypeStruct((B * T, POOL_WIDTH), F32),
        scratch_shapes=[pltpu.VMEM((Tc + POOL_BUF + 1, POOL_WIDTH), F32)],
        compiler_params=_cparams(("parallel", "arbitrary")),
        name="pool",
    )(P, prefix16, w_pool_bf, pool_scale.reshape(1, POOL_WIDTH))


def _merge_kernel(x_ref, g0_ref, g1_ref, g2_ref, oa_ref, o0_ref, o1_ref, o2_ref, m0_ref, m1_ref, m2_ref,
                  l0_ref, l1_ref, l2_ref, oc_ref, wa_ref, wb_ref, wc_ref, wo_ref, nf_ref, x1_ref, h2_ref, h2b_ref):
    m0, m1, m2 = m0_ref[...], m1_ref[...], m2_ref[...]
    mx = jnp.maximum(jnp.maximum(m0, m1), m2)
    e0, e1, e2 = jnp.exp(m0 - mx), jnp.exp(m1 - mx), jnp.exp(m2 - mx)
    num = e0 * o0_ref[...] + e1 * o1_ref[...] + e2 * o2_ref[...]
    den = e0 * l0_ref[...] + e1 * l1_ref[...] + e2 * l2_ref[...]
    ob = num / den

    def proj(a, w_ref):
        return jnp.dot(a.astype(BF16), w_ref[...], preferred_element_type=F32)

    merged = (_sigmoid(g0_ref[...]) * proj(oa_ref[...], wa_ref) + _sigmoid(g1_ref[...]) * proj(ob, wb_ref)
              + _sigmoid(g2_ref[...]) * proj(oc_ref[...], wc_ref))
    x1 = x_ref[...] + proj(merged, wo_ref)
    x1_ref[...] = x1
    h2 = x1 * lax.rsqrt(jnp.mean(x1 * x1, axis=-1, keepdims=True) + EPS) * nf_ref[...]
    h2_ref[...] = h2
    h2b_ref[...] = h2.astype(BF16)


def _merge(x, P, o_a, attn, o_c, wa, wb, wc, wo, norm_ffn, tm=256):
    n, d = x.shape

    def rows(width, col=0):
        return pl.BlockSpec((tm, width), lambda i, c=col: (i, c))

    def full(a):
        return pl.BlockSpec(a.shape, lambda i: (0,) * a.ndim)

    nf = norm_ffn.reshape(1, d)
    os_, ms_, ls_ = [a[0] for a in attn], [a[1] for a in attn], [a[2] for a in attn]
    g0 = COL_GATE // d
    return pl.pallas_call(
        _merge_kernel,
        grid=(n // tm,),
        in_specs=[rows(d), rows(d, g0), rows(d, g0 + 1), rows(d, g0 + 2), rows(o_a.shape[1])]
        + [rows(SWA_OUT)] * 9 + [rows(POOL_WIDTH), full(wa), full(wb), full(wc), full(wo), full(nf)],
        out_specs=[rows(d), rows(d), rows(d)],
        out_shape=[jax.ShapeDtypeStruct((n, d), F32), jax.ShapeDtypeStruct((n, d), F32),
                   jax.ShapeDtypeStruct((n, d), BF16)],
        compiler_params=_cparams(("parallel",)),
        name="merge",
    )(x, P, P, P, o_a, *os_, *ms_, *ls_, o_c, wa, wb, wc, wo, nf)


def _topk_rows(s, iota, k):
    vals, idxs = [], []
    big = float(s.shape[0])
    for _ in range(k):
        m = jnp.max(s, axis=0, keepdims=True)
        idx = jnp.min(jnp.where(s == m, iota, big), axis=0, keepdims=True)
        vals.append(m)
        idxs.append(idx)
        s = jnp.where(iota == idx, NEG_INF, s)
    return jnp.concatenate(vals, axis=0), jnp.concatenate(idxs, axis=0)


def _peer_topk_kernel(h_ref, wq_ref, sk_ref, ia_ref, ib_ref, g_ref, q_scr, ia_scr, ib_scr, g_scr, *, tn):
    TK = PEER_TOPK
    CH = 128
    q_scr[...] = jnp.dot(h_ref[...], wq_ref[...], precision=HI, preferred_element_type=F32)
    kio = lax.broadcasted_iota(jnp.int32, (PEER_KEYS, CH), 0).astype(F32)
    cio = lax.broadcasted_iota(jnp.int32, (TK * TK, CH), 0).astype(F32)

    def chunk(j, carry):
        r0 = pl.multiple_of(j * CH, CH)

        def head(h, carry2):
            c0 = pl.multiple_of(h * 2 * PEER_KEYS, 2 * PEER_KEYS)
            q1 = q_scr[pl.ds(r0, CH), pl.ds(c0, PEER_KEYS)]
            q2 = q_scr[pl.ds(r0, CH), pl.ds(c0 + PEER_KEYS, PEER_KEYS)]
            s1, i1 = _topk_rows(_mm_nt(sk_ref[0], q1), kio, TK)
            s2, i2 = _topk_rows(_mm_nt(sk_ref[1], q2), kio, TK)
            cand = jnp.concatenate([s1[r:r + 1] + s2 for r in range(TK)], axis=0)
            best, ic = _topk_rows(cand, cio, TK)
            rk = jnp.floor(ic * (1.0 / TK))
            ck = ic - rk * TK
            ia = jnp.zeros((TK, CH), F32)
            ib = jnp.zeros((TK, CH), F32)
            for r in range(TK):
                ia = jnp.where(rk == float(r), i1[r:r + 1], ia)
                ib = jnp.where(ck == float(r), i2[r:r + 1], ib)
            e = jnp.exp(best - best[0:1])
            gate = e / jnp.sum(e, axis=0, keepdims=True)
            o0 = pl.multiple_of(h * TK, TK)
            ia_scr[pl.ds(o0, TK), :] = ia
            ib_scr[pl.ds(o0, TK), :] = ib
            g_scr[pl.ds(o0, TK), :] = gate
            return carry2

        lax.fori_loop(0, PEER_HEADS, head, 0)
        ia_ref[pl.ds(r0, CH), :] = ia_scr[...].T.astype(jnp.int32)
        ib_ref[pl.ds(r0, CH), :] = ib_scr[...].T.astype(jnp.int32)
        g_ref[pl.ds(r0, CH), :] = g_scr[...].T
        return carry

    lax.fori_loop(0, tn // 128, chunk, 0)


def _peer_topk(h2, w_query, subkeys, tn=256):
    n, d = h2.shape
    nq = w_query.shape[1]
    out = pl.BlockSpec((tn, PEER_SEL), lambda i: (i, 0))
    return pl.pallas_call(
        functools.partial(_peer_topk_kernel, tn=tn),
        grid=(n // tn,),
        in_specs=[pl.BlockSpec((tn, d), lambda i: (i, 0)),
                  pl.BlockSpec((d, nq), lambda i: (0, 0)),
                  pl.BlockSpec(subkeys.shape, lambda i: (0, 0, 0))],
        out_specs=[out, out, out],
        out_shape=[jax.ShapeDtypeStruct((n, PEER_SEL), jnp.int32), jax.ShapeDtypeStruct((n, PEER_SEL), jnp.int32),
                   jax.ShapeDtypeStruct((n, PEER_SEL), F32)],
        scratch_shapes=[pltpu.VMEM((tn, nq), F32), pltpu.VMEM((PEER_SEL, 128), F32),
                        pltpu.VMEM((PEER_SEL, 128), F32), pltpu.VMEM((PEER_SEL, 128), F32)],
        compiler_params=_cparams(("parallel",)),
        name="peer_topk",
    )(h2, w_query, subkeys)


def _peer_up_kernel(h_ref, w_ref, o_ref):
    o_ref[...] = jnp.dot(h_ref[...], w_ref[...], preferred_element_type=F32).astype(o_ref.dtype)


def _peer_up(h2b, w_upT, tm=1024, tn=1024):
    n, d = h2b.shape
    e = w_upT.shape[1]
    return pl.pallas_call(
        _peer_up_kernel,
        grid=(n // tm, e // tn),
        in_specs=[pl.BlockSpec((tm, d), lambda i, j: (i, 0)), pl.BlockSpec((d, tn), lambda i, j: (0, j))],
        out_specs=pl.BlockSpec((tm, tn), lambda i, j: (i, j)),
        out_shape=jax.ShapeDtypeStruct((n, e), BF16),
        compiler_params=_cparams(("parallel", "arbitrary")),
        name="peer_up",
    )(h2b, w_upT)


def _peer_select_kernel(s3_ref, ia_ref, ib_ref, g_ref, a3_ref, act_scr, *, tn):
    sub = lax.broadcasted_iota(jnp.int32, (PEER_KEYS, PEER_SEL), 0)

    def gather(t, carry):
        ia = ia_ref[pl.ds(t, 1), :]
        ib = ib_ref[pl.ds(t, 1), :]
        onehot_b = jnp.where(sub == ib, 1.0, 0.0).astype(BF16)
        r = jnp.dot(s3_ref[t], onehot_b, preferred_element_type=F32)
        act_scr[pl.ds(t, 1), :] = jnp.sum(jnp.where(sub == ia, r, 0.0), axis=0, keepdims=True)
        return carry

    lax.fori_loop(0, tn, gather, 0)
    v = act_scr[...]
    act_scr[...] = g_ref[...] * (0.5 * v * (1.0 + lax.erf(v * (2.0 ** -0.5))))

    def scatter(t, carry):
        ia = ia_ref[pl.ds(t, 1), :]
        ib = ib_ref[pl.ds(t, 1), :]
        wa = jnp.where(sub == ia, act_scr[pl.ds(t, 1), :], 0.0).astype(BF16)
        onehot_b = jnp.where(sub == ib, 1.0, 0.0).astype(BF16)
        a3_ref[t] = lax.dot_general(wa, onehot_b, (((1,), (1,)), ((), ())),
                                    preferred_element_type=F32).astype(a3_ref.dtype)
        return carry

    lax.fori_loop(0, tn, scatter, 0)


def _peer_select(s3, ia, ib, gate, tn=128):
    n = s3.shape[0]
    big = pl.BlockSpec((tn, PEER_KEYS, PEER_KEYS), lambda i: (i, 0, 0))
    small = pl.BlockSpec((tn, PEER_SEL), lambda i: (i, 0))
    return pl.pallas_call(
        functools.partial(_peer_select_kernel, tn=tn),
        grid=(n // tn,),
        in_specs=[big, small, small, small],
        out_specs=big,
        out_shape=jax.ShapeDtypeStruct(s3.shape, BF16),
        scratch_shapes=[pltpu.VMEM((tn, PEER_SEL), F32)],
        compiler_params=_cparams(("parallel",)),
        name="peer_select",
    )(s3, ia, ib, gate)


def _peer_down_kernel(a_ref, w_ref, x_ref, g_ref, o_ref, acc_scr, *, final_norm):
    k = pl.program_id(1)

    @pl.when(k == 0)
    def _():
        acc_scr[...] = jnp.zeros(acc_scr.shape, F32)

    acc_scr[...] += jnp.dot(a_ref[...], w_ref[...], preferred_element_type=F32)

    @pl.when(k == pl.num_programs(1) - 1)
    def _():
        x = x_ref[...] + acc_scr[...]
        if final_norm:
            x = x * lax.rsqrt(jnp.mean(x * x, axis=-1, keepdims=True) + EPS) * g_ref[...]
        o_ref[...] = x


def _peer_down(a, w_down_bf, x1, norm_g, final_norm, tm=1024, tk=1024):
    n, e = a.shape
    d = w_down_bf.shape[1]
    return pl.pallas_call(
        functools.partial(_peer_down_kernel, final_norm=final_norm),
        grid=(n // tm, e // tk),
        in_specs=[pl.BlockSpec((tm, tk), lambda i, k: (i, k)), pl.BlockSpec((tk, d), lambda i, k: (k, 0)),
                  pl.BlockSpec((tm, d), lambda i, k: (i, 0)), pl.BlockSpec((1, d), lambda i, k: (0, 0))],
        out_specs=pl.BlockSpec((tm, d), lambda i, k: (i, 0)),
        out_shape=jax.ShapeDtypeStruct((n, d), F32),
        scratch_shapes=[pltpu.VMEM((tm, d), F32)],
        compiler_params=_cparams(("parallel", "arbitrary")),
        name="peer_down",
    )(a, w_down_bf, x1, norm_g.reshape(1, d))


def _peer(x1, h2, h2b, w_query, subkeys, w_up, w_down, norm_g, final_norm):
    n = x1.shape[0]
    ia, ib, gate = _peer_topk(h2, w_query, subkeys)
    s = _peer_up(h2b, w_up.T.astype(BF16))
    a3 = _peer_select(s.reshape(n, PEER_KEYS, PEER_KEYS), ia, ib, gate)
    return _peer_down(a3.reshape(n, PEER_KEYS * PEER_KEYS), w_down.astype(BF16), x1, norm_g, final_norm)


def kernel(x_prompt, x_sample, state_delta, state_conv, cache_win0, cache_win1, cache_win2, state_pool, norm_mix, w_in, conv_w, a_log, dt_bias, dn_norm, w_pool, pool_scale, w_branch, w_out, norm_ffn, peer_query, peer_subkeys, peer_up, peer_down, norm_final):
    Bp, Tp, D = x_prompt.shape
    Bs, Ts, _ = x_sample.shape
    depth = w_in.shape[0]
    past = PAST_LEN
    npr = Bp * Tp
    caches = (cache_win0, cache_win1, cache_win2)
    x = jnp.concatenate([x_prompt.reshape(npr, D), x_sample.reshape(Bs * Ts, D)], axis=0)
    cos_p, sin_p = _rope_tables(jnp.arange(Tp))
    cos_s, sin_s = _rope_tables(past + jnp.arange(Ts))
    nconv = CONV_W - 1
    dn_w = DN_HEADS * DN_DIM

    delta_p, delta_s, conv_p, conv_s, pool_p, pool_s = [], [], [], [], [], []
    win_p = [[] for _ in SWA_GROUPS]
    win_s = [[] for _ in SWA_GROUPS]
    for l in range(depth):
        P = _inproj(x, norm_mix[l], _pack_w_in(w_in[l]))
        Pp = P[:npr].reshape(Bp, Tp, P_COLS)
        Ps = P[npr:].reshape(Bs, Ts, P_COLS)
        cw8 = jnp.pad(conv_w[l], ((0, 8 - CONV_W), (0, 0)))

        abT_p = jnp.transpose(Pp[:, :, COL_AB:COL_AB + 8], (0, 2, 1))
        abT_s = jnp.transpose(Ps[:, :, COL_AB:COL_AB + 8], (0, 2, 1))
        oa_p, dp = _deltanet(P, abT_p, jnp.zeros((Bp, 8, 3 * dn_w), F32), cw8,
                             jnp.zeros((Bp, DN_HEADS, DN_DIM, DN_DIM), F32), a_log[l], dt_bias[l], dn_norm[l],
                             0, Bp, Tp, 256, DN_CHUNK)
        oa_s, ds = _deltanet(P, abT_s, jnp.pad(state_conv[l], ((0, 0), (8 - nconv, 0), (0, 0))), cw8,
                             state_delta[l], a_log[l], dt_bias[l], dn_norm[l], npr, Bs, Ts, Ts, min(DN_CHUNK, Ts))
        delta_p.append(dp)
        delta_s.append(ds)
        conv_p.append(Pp[:, Tp - nconv:, COL_QKV_A:COL_QKV_A + 3 * dn_w])
        conv_s.append(jnp.concatenate([state_conv[l], Ps[:, :, COL_QKV_A:COL_QKV_A + 3 * dn_w]], axis=1)[:, -nconv:])

        attn = []
        for gi, (window, dil) in enumerate(SWA_GROUPS):
            o_p, m_p, l_p, kr_p = _attn_prompt(P, cos_p, sin_p, gi, Bp, Tp)
            o_s, m_s, l_s, kr_s = _attn_sample(P, cos_s, sin_s, caches[gi][l], gi, npr, Bs, Ts)
            attn.append(tuple(jnp.concatenate(pair, axis=0) for pair in ((o_p, o_s), (m_p, m_s), (l_p, l_s))))
            vcol = COL_QKV_B + 2 * 3 * SWA_OUT + gi * SWA_OUT
            keep = min(window, Tp)
            kv_p = jnp.stack([kr_p.reshape(Bp, Tp, SWA_OUT)[:, Tp - keep:], Pp[:, Tp - keep:, vcol:vcol + SWA_OUT]], axis=2)
            win_p[gi].append(kv_p.reshape(Bp, keep, 2, SWA_HEADS, SWA_DIM))
            kv_s = jnp.stack([kr_s.reshape(Bs, Ts, SWA_OUT), Ps[:, :, vcol:vcol + SWA_OUT]], axis=2)
            kv_all = jnp.concatenate([caches[gi][l], kv_s.reshape(Bs, Ts, 2, SWA_HEADS, SWA_DIM)], axis=1)
            keep = min(window, kv_all.shape[1])
            win_s[gi].append(kv_all[:, kv_all.shape[1] - keep:])

        wp_bf = w_pool[l].astype(BF16)
        oc_p = _pool(P, jnp.zeros((Bp, POOL_BUF + 1, POOL_WIDTH), F32), wp_bf, pool_scale[l], 0, Bp, Tp, 256, 0)
        oc_s = _pool(P, jnp.pad(state_pool[l], ((0, 0), (1, 0), (0, 0))), wp_bf, pool_scale[l], npr, Bs, Ts, Ts, past)
        pool_p.append(Pp[:, Tp - POOL_BUF:, COL_U:COL_U + POOL_WIDTH])
        pool_s.append(jnp.concatenate([state_pool[l], Ps[:, :, COL_U:COL_U + POOL_WIDTH]], axis=1)[:, -POOL_BUF:])

        wb = w_branch[l].astype(BF16)
        x1, h2, h2b = _merge(x, P, jnp.concatenate([oa_p, oa_s], axis=0), attn, jnp.concatenate([oc_p, oc_s], axis=0),
                             wb[:dn_w], wb[dn_w:dn_w + SWA_OUT], wb[dn_w + SWA_OUT:], w_out[l].astype(BF16), norm_ffn[l])
        x = _peer(x1, h2, h2b, peer_query[l], peer_subkeys[l], peer_up[l], peer_down[l], norm_final, l == depth - 1)

    y_p = x[:npr].reshape(Bp, Tp, D)
    y_s = x[npr:].reshape(Bs, Ts, D)
    st = jnp.stack
    return (y_p, y_s, st(delta_p), st(delta_s), st(conv_p), st(conv_s),
            st(win_p[0]), st(win_s[0]), st(win_p[1]), st(win_s[1]), st(win_p[2]), st(win_s[2]), st(pool_p), st(pool_s))
```

```python
import functools

import jax
import jax.numpy as jnp
import numpy as np
from jax import lax
from jax.experimental import pallas as pl
from jax.experimental.pallas import tpu as pltpu

F32 = jnp.float32
BF16 = jnp.bfloat16
HI = lax.Precision.HIGHEST
EPS = 1e-6
NEG_INF = float("-inf")

D_MODEL = 1024
DN_HEADS = 4
DN_DIM = 128
DN_CHUNK = 64
CONV_W = 4
SWA_GROUPS = ((128, 1), (512, 4), (2048, 16))
SWA_HEADS = 4
SWA_DIM = 64
SWA_OUT = SWA_HEADS * SWA_DIM
SWA_BLOCK = 128
ROPE_THETA = 10000.0
POOL_WINDOWS = (2, 4, 8, 16)
POOL_GROUP = 384
POOL_WIDTH = 1536
POOL_BUF = 15
PEER_KEYS = 128
PEER_HEADS = 8
PEER_TOPK = 16
PEER_SEL = PEER_HEADS * PEER_TOPK
PAST_LEN = 2048

COL_QKV_A = 0
COL_U = 1536
COL_GATE = 3072
COL_Z = 6144
COL_QKV_B = 6656
COL_AB = 8960
P_COLS = 9216
VMEM_LIMIT = 56 * 1024 * 1024


def _cparams(sem):
    return pltpu.CompilerParams(dimension_semantics=sem, vmem_limit_bytes=VMEM_LIMIT)


def _sigmoid(x):
    return 1.0 / (1.0 + jnp.exp(-x))


def _silu(x):
    return x * _sigmoid(x)


def _inproj_kernel(x_ref, g_ref, w_ref, o_ref, h_scr):
    @pl.when(pl.program_id(1) == 0)
    def _():
        x = x_ref[...]
        y = x * lax.rsqrt(jnp.mean(x * x, axis=-1, keepdims=True) + EPS)
        h_scr[...] = (y * g_ref[...]).astype(BF16)

    o_ref[...] = jnp.dot(h_scr[...], w_ref[...], preferred_element_type=F32)


def _inproj(x, g, w_bf, tm=512, tn=1024):
    n, d = x.shape
    nc = w_bf.shape[1]
    return pl.pallas_call(
        _inproj_kernel,
        grid=(n // tm, nc // tn),
        in_specs=[pl.BlockSpec((tm, d), lambda i, j: (i, 0)),
                  pl.BlockSpec((1, d), lambda i, j: (0, 0)),
                  pl.BlockSpec((d, tn), lambda i, j: (0, j))],
        out_specs=pl.BlockSpec((tm, tn), lambda i, j: (i, j)),
        out_shape=jax.ShapeDtypeStruct((n, nc), F32),
        scratch_shapes=[pltpu.VMEM((tm, d), BF16)],
        compiler_params=_cparams(("parallel", "arbitrary")),
        name="inproj",
    )(x, g.reshape(1, d), w_bf)


def _pack_w_in(w):
    d = w.shape[0]
    qkv_b = w[:, 2056:4360].reshape(d, 3, len(SWA_GROUPS), SWA_OUT).transpose(0, 2, 1, 3).reshape(d, 2304)
    parts = [w[:, 0:1536], w[:, 4360:5896], w[:, 5896:8968], w[:, 1536:2048], qkv_b, w[:, 2048:2056],
             jnp.zeros((d, P_COLS - 8968), w.dtype)]
    return jnp.concatenate(parts, axis=1).astype(BF16)


def _qkvb_col(gi, which):
    return COL_QKV_B + (gi * 3 + which) * SWA_OUT


def _dn_kernel(alog_ref, dtb_ref, q_ref, k_ref, v_ref, z_ref, ab_ref, abT_ref, c0q_ref, c0k_ref, c0v_ref,
               cwq_ref, cwk_ref, cwv_ref, s0_ref, dnn_ref, o_ref, sout_ref, S_scr, ext_scr, *, C, Tc):
    h = pl.program_id(1)
    t = pl.program_id(2)

    @pl.when(t == 0)
    def _():
        S_scr[...] = s0_ref[0, 0]
        ext_scr[0, 0:8, :] = c0q_ref[0]
        ext_scr[1, 0:8, :] = c0k_ref[0]
        ext_scr[2, 0:8, :] = c0v_ref[0]

    def conv_act(i, x_ref, cw_ref):
        x = x_ref[...]
        ext_scr[i, 8:8 + Tc, :] = x
        w = cw_ref[...]
        acc = (ext_scr[i, 5:5 + Tc, :] * w[0:1] + ext_scr[i, 6:6 + Tc, :] * w[1:2]
               + ext_scr[i, 7:7 + Tc, :] * w[2:3] + x * w[3:4])
        ext_scr[i, 0:8, :] = ext_scr[i, Tc:Tc + 8, :]
        return _silu(acc)

    def l2n(x):
        return x * lax.rsqrt(jnp.sum(x * x, axis=-1, keepdims=True) + EPS)

    q_all = l2n(conv_act(0, q_ref, cwq_ref)) * (DN_DIM ** -0.5)
    k_all = l2n(conv_act(1, k_ref, cwk_ref))
    v_all = conv_act(2, v_ref, cwv_ref)

    ab = ab_ref[...]
    lane = lax.broadcasted_iota(jnp.int32, ab.shape, 1)
    a_col = jnp.sum(jnp.where(lane == h, ab, 0.0), axis=1, keepdims=True)
    b_col = jnp.sum(jnp.where(lane == h + DN_HEADS, ab, 0.0), axis=1, keepdims=True)
    abT = abT_ref[0]
    sub = lax.broadcasted_iota(jnp.int32, abT.shape, 0)
    a_row = jnp.sum(jnp.where(sub == h, abT, 0.0), axis=0, keepdims=True)

    neg_a = -jnp.exp(jnp.full((1, 1), alog_ref[h], F32))
    dtb = dtb_ref[h]

    def softplus(x):
        return jnp.maximum(x, 0.0) + jnp.log1p(jnp.exp(-jnp.abs(x)))

    g_col_all = neg_a * softplus(a_col + dtb)
    g_row_all = neg_a * softplus(a_row + dtb)
    beta_all = _sigmoid(b_col)
    zs = _silu(z_ref[...])
    dnn = dnn_ref[...]

    ii = lax.broadcasted_iota(jnp.int32, (C, C), 0)
    jj = lax.broadcasted_iota(jnp.int32, (C, C), 1)
    incl = ii >= jj
    strict = ii > jj
    eye = jnp.where(ii == jj, 1.0, 0.0).astype(F32)

    def mm(a, b):
        return jnp.dot(a, b, precision=HI, preferred_element_type=F32)

    def mm_nt(a, b):
        return lax.dot_general(a, b, (((1,), (1,)), ((), ())), precision=HI, preferred_element_type=F32)

    def mm_tn(a, b):
        return lax.dot_general(a, b, (((0,), (0,)), ((), ())), precision=HI, preferred_element_type=F32)

    S = S_scr[...]
    for c in range(Tc // C):
        sl = slice(c * C, (c + 1) * C)
        qc, kc, vc = q_all[sl], k_all[sl], v_all[sl]
        g_col, g_row, beta = g_col_all[sl], g_row_all[:, sl], beta_all[sl]
        gc_col = jnp.sum(jnp.where(incl, g_row, 0.0), axis=1, keepdims=True)
        gc_row = jnp.sum(jnp.where(ii <= jj, g_col, 0.0), axis=0, keepdims=True)
        gc_last = jnp.sum(g_row, axis=1, keepdims=True)
        decay = jnp.where(incl, jnp.exp(jnp.where(incl, gc_col - gc_row, 0.0)), 0.0)
        kb = kc * beta
        vb = vc * beta
        egc = jnp.exp(gc_col)
        lower = mm_nt(kb, kc) * jnp.where(strict, decay, 0.0)
        pw = -lower
        inv = eye + pw
        m = 2
        while m < C:
            pw = mm(pw, pw)
            inv = inv + mm(inv, pw)
            m *= 2
        u = mm(inv, vb)
        w = mm(inv, kb * egc)
        intra = mm_nt(qc, kc) * decay
        q_dec = qc * egc
        k_dec = kc * jnp.exp(gc_last - gc_col)
        v_new = u - mm(w, S)
        o = mm(q_dec, S) + mm(intra, v_new)
        S = S * jnp.exp(gc_last) + mm_tn(k_dec, v_new)
        o = o * lax.rsqrt(jnp.mean(o * o, axis=-1, keepdims=True) + EPS) * dnn
        o_ref[sl, :] = o * zs[sl]
    S_scr[...] = S

    @pl.when(t == pl.num_programs(2) - 1)
    def _():
        sout_ref[0, 0] = S


def _deltanet(P, abT, conv0p, cw8, state0, a_log, dt_bias, dn_norm, row0, B, T, Tc, C):
    n = P.shape[0]
    nt = T // Tc
    rb0 = row0 // Tc
    H = DN_HEADS

    def rows(col0):
        return pl.BlockSpec((Tc, 128), lambda b, h, t, c=col0: (rb0 + b * nt + t, c + h))

    def c0(col0):
        return pl.BlockSpec((1, 8, 128), lambda b, h, t, c=col0: (b, 0, c + h))

    def cw(col0):
        return pl.BlockSpec((8, 128), lambda b, h, t, c=col0: (0, c + h))

    smem = pl.BlockSpec(memory_space=pltpu.SMEM)
    o, s_new = pl.pallas_call(
        functools.partial(_dn_kernel, C=C, Tc=Tc),
        grid=(B, H, nt),
        in_specs=[smem, smem, rows(0), rows(4), rows(8), rows(COL_Z // 128),
                  pl.BlockSpec((Tc, 128), lambda b, h, t: (rb0 + b * nt + t, COL_AB // 128)),
                  pl.BlockSpec((1, 8, Tc), lambda b, h, t: (b, 0, t)),
                  c0(0), c0(4), c0(8), cw(0), cw(4), cw(8),
                  pl.BlockSpec((1, 1, 128, 128), lambda b, h, t: (b, h, 0, 0)),
                  pl.BlockSpec((1, 128), lambda b, h, t: (0, 0))],
        out_specs=[pl.BlockSpec((Tc, 128), lambda b, h, t: (b * nt + t, h)),
                   pl.BlockSpec((1, 1, 128, 128), lambda b, h, t: (b, h, 0, 0))],
        out_shape=[jax.ShapeDtypeStruct((B * T, H * 128), F32),
                   jax.ShapeDtypeStruct((B, H, 128, 128), F32)],
        scratch_shapes=[pltpu.VMEM((128, 128), F32), pltpu.VMEM((3, Tc + 8, 128), F32)],
        compiler_params=_cparams(("parallel", "parallel", "arbitrary")),
        name="deltanet",
    )(a_log, dt_bias, P, P, P, P, P, abT, conv0p, conv0p, conv0p, cw8, cw8, cw8, state0, dn_norm.reshape(1, 128))
    return o, s_new


def _dn4_kernel(alog_ref, dtb_ref, q_ref, k_ref, v_ref, z_ref, ab_ref, abT_ref, c0_ref, cw_ref, s0_ref, dnn_ref,
                o_ref, sout_ref, S_scr, ext_scr, *, C, Tc):
    H, DH = DN_HEADS, DN_DIM
    W = H * DH
    t = pl.program_id(1)

    @pl.when(t == 0)
    def _():
        S_scr[...] = s0_ref[0]
        ext_scr[0:8, :] = c0_ref[0]

    ext_scr[8:8 + Tc, 0:W] = q_ref[...]
    ext_scr[8:8 + Tc, W:2 * W] = k_ref[...]
    ext_scr[8:8 + Tc, 2 * W:3 * W] = v_ref[...]
    cw = cw_ref[...]
    act = _silu(ext_scr[5:5 + Tc, :] * cw[0:1] + ext_scr[6:6 + Tc, :] * cw[1:2]
                + ext_scr[7:7 + Tc, :] * cw[2:3] + ext_scr[8:8 + Tc, :] * cw[3:4])
    ext_scr[0:8, :] = ext_scr[Tc:Tc + 8, :]

    def l2n(x):
        return x * lax.rsqrt(jnp.sum(x * x, axis=-1, keepdims=True) + EPS)

    def softplus(x):
        return jnp.maximum(x, 0.0) + jnp.log1p(jnp.exp(-jnp.abs(x)))

    def mm(a, b):
        return jnp.dot(a, b, precision=HI, preferred_element_type=F32)

    def mm_nt(a, b):
        return lax.dot_general(a, b, (((1,), (1,)), ((), ())), precision=HI, preferred_element_type=F32)

    def mm_tn(a, b):
        return lax.dot_general(a, b, (((0,), (0,)), ((), ())), precision=HI, preferred_element_type=F32)

    ab = ab_ref[...]
    abT = abT_ref[0]
    zs = _silu(z_ref[...])
    dnn = dnn_ref[...]
    ii = lax.broadcasted_iota(jnp.int32, (C, C), 0)
    jj = lax.broadcasted_iota(jnp.int32, (C, C), 1)
    incl = ii >= jj
    strict = ii > jj
    eye = jnp.where(ii == jj, 1.0, 0.0).astype(F32)
    nck = Tc // C

    pairs = []
    for h in range(H):
        qh = l2n(act[:, h * DH:(h + 1) * DH]) * (DH ** -0.5)
        kh = l2n(act[:, W + h * DH:W + (h + 1) * DH])
        vh = act[:, 2 * W + h * DH:2 * W + (h + 1) * DH]
        neg_a = -jnp.exp(jnp.full((1, 1), alog_ref[h], F32))
        g_col_h = neg_a * softplus(ab[:, h:h + 1] + dtb_ref[h])
        g_row_h = neg_a * softplus(abT[h:h + 1, :] + dtb_ref[h])
        beta_h = _sigmoid(ab[:, H + h:H + h + 1])
        for c in range(nck):
            sl = slice(c * C, (c + 1) * C)
            qc, kc, vc = qh[sl], kh[sl], vh[sl]
            g_col, g_row, beta = g_col_h[sl], g_row_h[:, sl], beta_h[sl]
            gc_col = jnp.sum(jnp.where(incl, g_row, 0.0), axis=1, keepdims=True)
            gc_row = jnp.sum(jnp.where(ii <= jj, g_col, 0.0), axis=0, keepdims=True)
            gc_last = jnp.sum(g_row, axis=1, keepdims=True)
            decay = jnp.where(incl, jnp.exp(jnp.where(incl, gc_col - gc_row, 0.0)), 0.0)
            kb = kc * beta
            egc = jnp.exp(gc_col)
            pairs.append(dict(h=h, c=c, sl=sl, qc=qc, kc=kc, kb=kb, vb=vc * beta, decay=decay, egc=egc,
                              kdec=kc * jnp.exp(gc_last - gc_col), glast=jnp.exp(gc_last)))
    for p in pairs:
        p["pw"] = -(mm_nt(p["kb"], p["kc"]) * jnp.where(strict, p["decay"], 0.0))
        p["intra"] = mm_nt(p["qc"], p["kc"]) * p["decay"]
    for p in pairs:
        p["inv"] = eye + p["pw"]
    m = 2
    while m < C:
        for p in pairs:
            p["pw"] = mm(p["pw"], p["pw"])
        for p in pairs:
            p["inv"] = p["inv"] + mm(p["inv"], p["pw"])
        m *= 2
    for p in pairs:
        p["u"] = mm(p["inv"], p["vb"])
        p["w"] = mm(p["inv"], p["kb"] * p["egc"])
    S = [S_scr[h] for h in range(H)]
    for c in range(nck):
        cur = [p for p in pairs if p["c"] == c]
        vnew = [p["u"] - mm(p["w"], S[p["h"]]) for p in cur]
        for p, vn in zip(cur, vnew):
            h = p["h"]
            o = mm(p["qc"] * p["egc"], S[h]) + mm(p["intra"], vn)
            S[h] = S[h] * p["glast"] + mm_tn(p["kdec"], vn)
            o = o * lax.rsqrt(jnp.mean(o * o, axis=-1, keepdims=True) + EPS) * dnn
            o_ref[p["sl"], h * DH:(h + 1) * DH] = o * zs[p["sl"], h * DH:(h + 1) * DH]
    for h in range(H):
        S_scr[h] = S[h]

    @pl.when(t == pl.num_programs(1) - 1)
    def _():
        for h in range(H):
            sout_ref[0, h] = S[h]


def _deltanet4(P, abT, conv0p, cw8, state0, a_log, dt_bias, dn_norm, row0, B, T, Tc, C):
    nt = T // Tc
    rb0 = row0 // Tc
    H, W = DN_HEADS, DN_HEADS * DN_DIM

    def rows(width, col):
        return pl.BlockSpec((Tc, width), lambda b, t, c=col: (rb0 + b * nt + t, c))

    smem = pl.BlockSpec(memory_space=pltpu.SMEM)
    return pl.pallas_call(
        functools.partial(_dn4_kernel, C=C, Tc=Tc),
        grid=(B, nt),
        in_specs=[smem, smem, rows(W, 0), rows(W, 1), rows(W, 2), rows(W, COL_Z // W), rows(128, COL_AB // 128),
                  pl.BlockSpec((1, 8, Tc), lambda b, t: (b, 0, t)),
                  pl.BlockSpec((1, 8, 3 * W), lambda b, t: (b, 0, 0)),
                  pl.BlockSpec((8, 3 * W), lambda b, t: (0, 0)),
                  pl.BlockSpec((1, H, DN_DIM, DN_DIM), lambda b, t: (b, 0, 0, 0)),
                  pl.BlockSpec((1, DN_DIM), lambda b, t: (0, 0))],
        out_specs=[pl.BlockSpec((Tc, W), lambda b, t: (b * nt + t, 0)),
                   pl.BlockSpec((1, H, DN_DIM, DN_DIM), lambda b, t: (b, 0, 0, 0))],
        out_shape=[jax.ShapeDtypeStruct((B * T, W), F32), jax.ShapeDtypeStruct((B, H, DN_DIM, DN_DIM), F32)],
        scratch_shapes=[pltpu.VMEM((H, DN_DIM, DN_DIM), F32), pltpu.VMEM((Tc + 8, 3 * W), F32)],
        compiler_params=_cparams(("parallel", "arbitrary")),
        name="deltanet",
    )(a_log, dt_bias, P, P, P, P, P, abT, conv0p, cw8, state0, dn_norm.reshape(1, DN_DIM))


def _rope_tables(pos):
    half = SWA_DIM // 2
    inv = ROPE_THETA ** (-jnp.arange(half, dtype=F32) / half)
    ang = pos.astype(F32)[:, None] * inv[None, :]
    cos = jnp.tile(jnp.cos(ang), (1, 2 * SWA_HEADS))
    sin = jnp.tile(jnp.sin(ang), (1, 2 * SWA_HEADS))
    return cos, sin


def _rope(x, cos, sin):
    lane = lax.broadcasted_iota(jnp.int32, x.shape, 1)
    first = (lane % SWA_DIM) < (SWA_DIM // 2)
    w = x.shape[1]
    xr = jnp.where(first, -pltpu.roll(x, w - SWA_DIM // 2, 1), pltpu.roll(x, SWA_DIM // 2, 1))
    return x * cos + xr * sin


def _mm_nt(a, b):
    return lax.dot_general(a, b, (((1,), (1,)), ((), ())), precision=HI, preferred_element_type=F32)


def _mm(a, b):
    return jnp.dot(a, b, precision=HI, preferred_element_type=F32)


def _attn_prompt_kernel(q_ref, kc_ref, kp_ref, vc_ref, vp_ref, cq_ref, sq_ref, cp_ref, sp_ref,
                        o_ref, m_ref, l_ref, kr_ref, *, n_back):
    i = pl.program_id(2)
    Bq = SWA_BLOCK
    q = _rope(q_ref[...], cq_ref[...], sq_ref[...])
    k_cur = _rope(kc_ref[...], cq_ref[...], sq_ref[...])
    k_prev = _rope(kp_ref[...], cp_ref[...], sp_ref[...])
    kr_ref[...] = k_cur
    kcat = jnp.concatenate([k_prev, k_cur], axis=0)
    vcat = jnp.concatenate([vp_ref[...], vc_ref[...]], axis=0)
    H = SWA_HEADS
    qi = lax.broadcasted_iota(jnp.int32, (H * Bq, 2 * Bq), 0) % Bq
    kj = lax.broadcasted_iota(jnp.int32, (H * Bq, 2 * Bq), 1)
    rel = Bq + qi - kj
    valid = (rel >= 0) & (rel <= n_back) & (i * Bq + qi - rel >= 0)
    head = lax.broadcasted_iota(jnp.int32, (Bq, SWA_OUT), 1) // SWA_DIM
    qs = jnp.concatenate([jnp.where(head == h, q, 0.0) for h in range(H)], axis=0).astype(BF16)
    s = lax.dot_general(qs, kcat.astype(BF16), (((1,), (1,)), ((), ())), preferred_element_type=F32)
    s = jnp.where(valid, s * (SWA_DIM ** -0.5), NEG_INF)
    m = jnp.max(s, axis=1, keepdims=True)
    p = jnp.exp(s - m)
    l = jnp.sum(p, axis=1, keepdims=True)
    acc = jnp.dot(p.astype(BF16), vcat.astype(BF16), preferred_element_type=F32)
    o = jnp.zeros((Bq, SWA_OUT), F32)
    m_full = jnp.zeros((Bq, SWA_OUT), F32)
    l_full = jnp.zeros((Bq, SWA_OUT), F32)
    for h in range(H):
        rows = slice(h * Bq, (h + 1) * Bq)
        o = jnp.where(head == h, acc[rows], o)
        m_full = jnp.where(head == h, m[rows], m_full)
        l_full = jnp.where(head == h, l[rows], l_full)
    o_ref[...] = o
    m_ref[...] = m_full
    l_ref[...] = l_full


def _attn_prompt(P, cos, sin, gi, B, T):
    window, dil = SWA_GROUPS[gi]
    n_back = window // dil
    assert n_back <= SWA_BLOCK
    L = T // dil
    nb = L // SWA_BLOCK
    if dil == 1:
        Pd, cw, c0 = P, 0, _qkvb_col(gi, 0) // SWA_OUT
    else:
        Pd = P[:B * T, _qkvb_col(gi, 0):_qkvb_col(gi, 0) + 3 * SWA_OUT].reshape(B * L, dil * 3 * SWA_OUT)
        cw, c0 = 3, 0
    cd = cos.reshape(L, dil * SWA_OUT)
    sd = sin.reshape(L, dil * SWA_OUT)
    blk = (SWA_BLOCK, SWA_OUT)

    def cur(which):
        return pl.BlockSpec(blk, lambda b, p, i, c=c0 + which: (b * nb + i, p * cw + c))

    def prev(which):
        return pl.BlockSpec(blk, lambda b, p, i, c=c0 + which: (b * nb + jnp.maximum(i - 1, 0), p * cw + c))

    tcur = pl.BlockSpec(blk, lambda b, p, i: (i, p))
    tprev = pl.BlockSpec(blk, lambda b, p, i: (jnp.maximum(i - 1, 0), p))
    ospec = pl.BlockSpec(blk, lambda b, p, i: (b * nb + i, p))
    oshape = jax.ShapeDtypeStruct((B * L, dil * SWA_OUT), F32)
    outs = pl.pallas_call(
        functools.partial(_attn_prompt_kernel, n_back=n_back),
        grid=(B, dil, nb),
        in_specs=[cur(0), cur(1), prev(1), cur(2), prev(2), tcur, tcur, tprev, tprev],
        out_specs=[ospec] * 4,
        out_shape=[oshape] * 4,
        compiler_params=_cparams(("parallel", "parallel", "arbitrary")),
        name=f"attn_prompt{gi}",
    )(Pd, Pd, Pd, Pd, Pd, cd, sd, cd, sd)
    return [a.reshape(B * T, SWA_OUT) for a in outs]


def _attn_sample_kernel(q_ref, k_ref, v_ref, cos_ref, sin_ref, kv_ref, o_ref, m_ref, l_ref, kr_ref, *, dil):
    T = q_ref.shape[0]
    W = kv_ref.shape[-1]
    H = SWA_HEADS
    q = _rope(q_ref[...], cos_ref[...], sin_ref[...])
    k_new = _rope(k_ref[...], cos_ref[...], sin_ref[...])
    v_new = v_ref[...]
    kr_ref[...] = k_new
    head = lax.broadcasted_iota(jnp.int32, (T, SWA_OUT), 1) // SWA_DIM
    tq = lax.broadcasted_iota(jnp.int32, (H * T, W), 0) % T
    wk = lax.broadcasted_iota(jnp.int32, (H * T, W), 1)
    valid_buf = (((W + tq - wk) & (dil - 1)) == 0) & (wk >= tq)
    tq2 = lax.broadcasted_iota(jnp.int32, (H * T, T), 0) % T
    tk2 = lax.broadcasted_iota(jnp.int32, (H * T, T), 1)
    valid_new = (((tq2 - tk2) & (dil - 1)) == 0) & (tk2 <= tq2)
    scale = SWA_DIM ** -0.5
    nt_dims = (((1,), (1,)), ((), ()))
    qs = jnp.concatenate([jnp.where(head == h, q, 0.0) for h in range(H)], axis=0).astype(BF16)
    k_t = kv_ref[0, 0, 0].reshape(H * SWA_DIM, W).astype(BF16)
    v_t = kv_ref[0, 0, 1].reshape(H * SWA_DIM, W).astype(BF16)
    sn = lax.dot_general(qs, k_new.astype(BF16), nt_dims, preferred_element_type=F32)
    sn = jnp.where(valid_new, sn * scale, NEG_INF)
    sb = jnp.where(valid_buf, jnp.dot(qs, k_t, preferred_element_type=F32) * scale, NEG_INF)
    m = jnp.maximum(jnp.max(sn, axis=1, keepdims=True), jnp.max(sb, axis=1, keepdims=True))
    pn = jnp.exp(sn - m)
    pb = jnp.exp(sb - m)
    l = jnp.sum(pn, axis=1, keepdims=True) + jnp.sum(pb, axis=1, keepdims=True)
    acc = (jnp.dot(pn.astype(BF16), v_new.astype(BF16), preferred_element_type=F32)
           + lax.dot_general(pb.astype(BF16), v_t, nt_dims, preferred_element_type=F32))
    o = jnp.zeros((T, SWA_OUT), F32)
    m_full = jnp.zeros((T, SWA_OUT), F32)
    l_full = jnp.zeros((T, SWA_OUT), F32)
    for h in range(H):
        rows = slice(h * T, (h + 1) * T)
        o = jnp.where(head == h, acc[rows], o)
        m_full = jnp.where(head == h, m[rows], m_full)
        l_full = jnp.where(head == h, l[rows], l_full)
    o_ref[...] = o
    m_ref[...] = m_full
    l_ref[...] = l_full


def _attn_sample(P, cos, sin, cache_t, layer, gi, row0, B, T):
    window, dil = SWA_GROUPS[gi]
    W = cache_t.shape[-1]
    assert W == window and W % dil == 0 and dil & (dil - 1) == 0 and row0 % T == 0
    rb0 = row0 // T
    blk = (T, SWA_OUT)

    def rows(which):
        return pl.BlockSpec(blk, lambda b, c=_qkvb_col(gi, which) // SWA_OUT: (rb0 + b, c))

    tab = pl.BlockSpec(blk, lambda b: (0, 0))
    ospec = pl.BlockSpec(blk, lambda b: (b, 0))
    oshape = jax.ShapeDtypeStruct((B * T, SWA_OUT), F32)
    return pl.pallas_call(
        functools.partial(_attn_sample_kernel, dil=dil),
        grid=(B,),
        in_specs=[rows(0), rows(1), rows(2), tab, tab,
                  pl.BlockSpec((1, 1, 2, SWA_HEADS, SWA_DIM, W), lambda b: (layer, b, 0, 0, 0, 0))],
        out_specs=[ospec] * 4,
        out_shape=[oshape] * 4,
        compiler_params=_cparams(("parallel",)),
        name=f"attn_sample{gi}",
    )(P, P, P, cos, sin, cache_t)


def _pool_kernel(u_ref, pre_ref, w_ref, sc_ref, o_ref, ext_scr, *, Tc, pos0):
    t = pl.program_id(1)
    nb = POOL_BUF + 1

    @pl.when(t == 0)
    def _():
        ext_scr[0:nb, :] = pre_ref[0]

    u = u_ref[...]
    ext_scr[nb:nb + Tc, :] = u
    pos = (pos0 + t * Tc + lax.broadcasted_iota(jnp.int32, (Tc, 1), 0)).astype(F32)
    for gi, w in enumerate(POOL_WINDOWS):
        lo, hi = gi * POOL_GROUP, (gi + 1) * POOL_GROUP
        ug = u[:, lo:hi]
        acc = ug
        for i in range(1, w):
            acc = acc + ext_scr[nb - i:nb - i + Tc, lo:hi]
        d = acc / jnp.minimum(float(w), pos + 1.0) - ug
        y = jnp.dot(d.astype(BF16), w_ref[gi], preferred_element_type=F32)
        o_ref[:, lo:hi] = y * sc_ref[:, lo:hi]
    ext_scr[0:nb, :] = ext_scr[Tc:Tc + nb, :]


def _pool(P, prefix16, w_pool_bf, pool_scale, row0, B, T, Tc, pos0):
    nt = T // Tc
    rb0 = row0 // Tc
    return pl.pallas_call(
        functools.partial(_pool_kernel, Tc=Tc, pos0=pos0),
        grid=(B, nt),
        in_specs=[pl.BlockSpec((Tc, POOL_WIDTH), lambda b, t: (rb0 + b * nt + t, COL_U // POOL_WIDTH)),
                  pl.BlockSpec((1, POOL_BUF + 1, POOL_WIDTH), lambda b, t: (b, 0, 0)),
                  pl.BlockSpec((len(POOL_WINDOWS), POOL_GROUP, POOL_GROUP), lambda b, t: (0, 0, 0)),
                  pl.BlockSpec((1, POOL_WIDTH), lambda b, t: (0, 0))],
        out_specs=pl.BlockSpec((Tc, POOL_WIDTH), lambda b, t: (b * nt + t, 0)),
        out_shape=jax.ShapeDtypeStruct((B * T, POOL_WIDTH), F32),
        scratch_shapes=[pltpu.VMEM((Tc + POOL_BUF + 1, POOL_WIDTH), F32)],
        compiler_params=_cparams(("parallel", "arbitrary")),
        name="pool",
    )(P, prefix16, w_pool_bf, pool_scale.reshape(1, POOL_WIDTH))


N_MIX = 11


def _merge_kernel(x_ref, g0_ref, g1_ref, g2_ref, *refs, np_blocks):
    mix_p, mix_s = refs[:N_MIX], refs[N_MIX:2 * N_MIX]
    wa_ref, wb_ref, wc_ref, wo_ref, nf_ref, x1_ref, h2_ref, h2b_ref = refs[2 * N_MIX:]
    is_p = pl.program_id(0) < np_blocks
    oa, o0, m0, l0, o1, m1, l1, o2, m2, l2, oc = [jnp.where(is_p, p[...], s[...]) for p, s in zip(mix_p, mix_s)]
    mx = jnp.maximum(jnp.maximum(m0, m1), m2)
    e0, e1, e2 = jnp.exp(m0 - mx), jnp.exp(m1 - mx), jnp.exp(m2 - mx)
    ob = (e0 * o0 + e1 * o1 + e2 * o2) / (e0 * l0 + e1 * l1 + e2 * l2)

    def proj(a, w_ref):
        return jnp.dot(a.astype(BF16), w_ref[...], preferred_element_type=F32)

    merged = (_sigmoid(g0_ref[...]) * proj(oa, wa_ref) + _sigmoid(g1_ref[...]) * proj(ob, wb_ref)
              + _sigmoid(g2_ref[...]) * proj(oc, wc_ref))
    x1 = x_ref[...] + proj(merged, wo_ref)
    x1_ref[...] = x1
    h2 = x1 * lax.rsqrt(jnp.mean(x1 * x1, axis=-1, keepdims=True) + EPS) * nf_ref[...]
    h2_ref[...] = h2
    h2b_ref[...] = h2.astype(BF16)


def _merge(x, P, mix_p, mix_s, wa, wb, wc, wo, norm_ffn, tm=256):
    n, d = x.shape
    npb = mix_p[0].shape[0] // tm
    nsb = mix_s[0].shape[0] // tm
    assert npb + nsb == n // tm

    def rows(width, col=0):
        return pl.BlockSpec((tm, width), lambda i, c=col: (i, c))

    def prow(a):
        return pl.BlockSpec((tm, a.shape[1]), lambda i: (jnp.minimum(i, npb - 1), 0))

    def srow(a):
        return pl.BlockSpec((tm, a.shape[1]), lambda i: (jnp.maximum(i - npb, 0), 0))

    def full(a):
        return pl.BlockSpec(a.shape, lambda i: (0,) * a.ndim)

    nf = norm_ffn.reshape(1, d)
    g0 = COL_GATE // d
    return pl.pallas_call(
        functools.partial(_merge_kernel, np_blocks=npb),
        grid=(n // tm,),
        in_specs=[rows(d), rows(d, g0), rows(d, g0 + 1), rows(d, g0 + 2)] + [prow(a) for a in mix_p]
        + [srow(a) for a in mix_s] + [full(wa), full(wb), full(wc), full(wo), full(nf)],
        out_specs=[rows(d), rows(d), rows(d)],
        out_shape=[jax.ShapeDtypeStruct((n, d), F32), jax.ShapeDtypeStruct((n, d), F32),
                   jax.ShapeDtypeStruct((n, d), BF16)],
        compiler_params=_cparams(("parallel",)),
        name="merge",
    )(x, P, P, P, *mix_p, *mix_s, wa, wb, wc, wo, nf)


def _topk_rows(s, iota, k):
    vals, idxs = [], []
    big = 1e9
    for _ in range(k):
        m = jnp.max(s, axis=0, keepdims=True)
        idx = jnp.min(jnp.where(s == m, iota, big), axis=0, keepdims=True)
        vals.append(m)
        idxs.append(idx)
        s = jnp.where(iota == idx, NEG_INF, s)
    return jnp.concatenate(vals, axis=0), jnp.concatenate(idxs, axis=0)


def _peer_topk_kernel(h_ref, wqh_ref, wql_ref, sk_ref, ia_ref, ib_ref, g_ref, q_scr, ia_scr, ib_scr, g_scr, *, tn):
    TK = PEER_TOPK
    CH = 128
    h = h_ref[...]
    h_hi = h.astype(BF16)
    h_lo = (h - h_hi.astype(F32)).astype(BF16)
    q_scr[...] = (jnp.dot(h_hi, wqh_ref[...], preferred_element_type=F32)
                  + jnp.dot(h_hi, wql_ref[...], preferred_element_type=F32)
                  + jnp.dot(h_lo, wqh_ref[...], preferred_element_type=F32))
    kio = lax.broadcasted_iota(jnp.int32, (PEER_KEYS, CH), 0).astype(F32)
    rho = lax.broadcasted_iota(jnp.int32, (TK + 8 * 8, CH), 0)
    mid = rho - TK
    cand_r = jnp.where(rho < TK, 0, jnp.where(mid < 56, (mid >> 3) + 1, mid - 56 + 8))
    cand_c = jnp.where(rho < TK, rho, jnp.where(mid < 56, mid & 7, 0))
    cand_ok = (cand_r + 1) * (cand_c + 1) <= TK
    cio = (cand_r * TK + cand_c).astype(F32)

    def chunk(j, carry):
        r0 = pl.multiple_of(j * CH, CH)

        def head(h, carry2):
            c0 = pl.multiple_of(h * 2 * PEER_KEYS, 2 * PEER_KEYS)
            q1 = q_scr[pl.ds(r0, CH), pl.ds(c0, PEER_KEYS)]
            q2 = q_scr[pl.ds(r0, CH), pl.ds(c0 + PEER_KEYS, PEER_KEYS)]
            s1, i1 = _topk_rows(_mm_nt(sk_ref[0], q1), kio, TK)
            s2, i2 = _topk_rows(_mm_nt(sk_ref[1], q2), kio, TK)
            cand = jnp.concatenate([s1[0:1] + s2] + [s1[r:r + 1] + s2[0:8] for r in range(1, 8)]
                                   + [s1[8:TK] + s2[0:1]], axis=0)
            best, ic = _topk_rows(jnp.where(cand_ok, cand, NEG_INF), cio, TK)
            rk = jnp.floor(ic * (1.0 / TK))
            ck = ic - rk * TK
            ia = jnp.zeros((TK, CH), F32)
            ib = jnp.zeros((TK, CH), F32)
            for r in range(TK):
                ia = jnp.where(rk == float(r), i1[r:r + 1], ia)
                ib = jnp.where(ck == float(r), i2[r:r + 1], ib)
            e = jnp.exp(best - best[0:1])
            gate = e / jnp.sum(e, axis=0, keepdims=True)
            o0 = pl.multiple_of(h * TK, TK)
            ia_scr[pl.ds(o0, TK), :] = ia
            ib_scr[pl.ds(o0, TK), :] = ib
            g_scr[pl.ds(o0, TK), :] = gate
            return carry2

        lax.fori_loop(0, PEER_HEADS, head, 0, unroll=2)
        ia_ref[pl.ds(r0, CH), :] = ia_scr[...].T.astype(jnp.int32)
        ib_ref[pl.ds(r0, CH), :] = ib_scr[...].T.astype(jnp.int32)
        g_ref[pl.ds(r0, CH), :] = g_scr[...].T
        return carry

    lax.fori_loop(0, tn // 128, chunk, 0)


def _peer_topk(h2, w_query, subkeys, tn=256):
    n, d = h2.shape
    nq = w_query.shape[1]
    wq_hi = w_query.astype(BF16)
    wq_lo = (w_query - wq_hi.astype(F32)).astype(BF16)
    out = pl.BlockSpec((tn, PEER_SEL), lambda i: (i, 0))
    return pl.pallas_call(
        functools.partial(_peer_topk_kernel, tn=tn),
        grid=(n // tn,),
        in_specs=[pl.BlockSpec((tn, d), lambda i: (i, 0)),
                  pl.BlockSpec((d, nq), lambda i: (0, 0)),
                  pl.BlockSpec((d, nq), lambda i: (0, 0)),
                  pl.BlockSpec(subkeys.shape, lambda i: (0, 0, 0))],
        out_specs=[out, out, out],
        out_shape=[jax.ShapeDtypeStruct((n, PEER_SEL), jnp.int32), jax.ShapeDtypeStruct((n, PEER_SEL), jnp.int32),
                   jax.ShapeDtypeStruct((n, PEER_SEL), F32)],
        scratch_shapes=[pltpu.VMEM((tn, nq), F32), pltpu.VMEM((PEER_SEL, 128), F32),
                        pltpu.VMEM((PEER_SEL, 128), F32), pltpu.VMEM((PEER_SEL, 128), F32)],
        compiler_params=_cparams(("parallel",)),
        name="peer_topk",
    )(h2, wq_hi, wq_lo, subkeys)


def _peer_up_kernel(h_ref, w_ref, o_ref):
    o_ref[...] = lax.dot_general(h_ref[...], w_ref[...], (((1,), (1,)), ((), ())),
                                 preferred_element_type=F32).astype(o_ref.dtype)


def _peer_up(h2b, w_up_bf, tm=1024, tn=1024):
    n, d = h2b.shape
    e = w_up_bf.shape[0]
    return pl.pallas_call(
        _peer_up_kernel,
        grid=(n // tm, e // tn),
        in_specs=[pl.BlockSpec((tm, d), lambda i, j: (i, 0)), pl.BlockSpec((tn, d), lambda i, j: (j, 0))],
        out_specs=pl.BlockSpec((tm, tn), lambda i, j: (i, j)),
        out_shape=jax.ShapeDtypeStruct((n, e), BF16),
        compiler_params=_cparams(("parallel", "arbitrary")),
        name="peer_up",
    )(h2b, w_up_bf)


def _peer_select_kernel(s_ref, ia_ref, ib_ref, g_ref, a_ref, x_scr, y_scr, *, tn, unroll):
    NK = PEER_KEYS
    SUB = 8

    def unpack(a8, carry):
        col = pl.multiple_of(a8 * SUB * NK, SUB * NK)
        x = jnp.stack([s_ref[:, pl.ds(col + i * NK, NK)].astype(F32) for i in range(SUB)], axis=0)
        x_scr[:, pl.ds(pl.multiple_of(a8 * SUB, SUB), SUB), :] = jnp.swapaxes(x, 0, 1)
        return carry

    lax.fori_loop(0, NK // SUB, unpack, 0)
    sub = lax.broadcasted_iota(jnp.int32, (NK, PEER_SEL), 0)

    def group(gidx, carry):
        toks = [gidx * unroll + u for u in range(unroll)]
        is_a = [sub == ia_ref[pl.ds(t, 1), :] for t in toks]
        onehot_b = [jnp.where(sub == ib_ref[pl.ds(t, 1), :], 1.0, 0.0).astype(BF16) for t in toks]
        vals = []
        for u, t in enumerate(toks):
            r = jnp.dot(x_scr[t].astype(BF16), onehot_b[u], preferred_element_type=F32)
            vals.append(jnp.sum(jnp.where(is_a[u], r, 0.0), axis=0, keepdims=True))
        for u, t in enumerate(toks):
            v = vals[u]
            act = g_ref[pl.ds(t, 1), :] * (0.5 * v * (1.0 + lax.erf(v * (2.0 ** -0.5))))
            wa = jnp.where(is_a[u], act, 0.0).astype(BF16)
            y_scr[t] = lax.dot_general(wa, onehot_b[u], (((1,), (1,)), ((), ())), preferred_element_type=F32)
        return carry

    lax.fori_loop(0, tn // unroll, group, 0)

    def pack(a8, carry):
        col = pl.multiple_of(a8 * SUB * NK, SUB * NK)
        y = jnp.swapaxes(y_scr[:, pl.ds(pl.multiple_of(a8 * SUB, SUB), SUB), :], 0, 1)
        for i in range(SUB):
            a_ref[:, pl.ds(col + i * NK, NK)] = y[i].astype(a_ref.dtype)
        return carry

    lax.fori_loop(0, NK // SUB, pack, 0)


def _peer_select(s, ia, ib, gate, tn=128, unroll=16):
    n, e = s.shape
    big = pl.BlockSpec((tn, e), lambda i: (i, 0))
    small = pl.BlockSpec((tn, PEER_SEL), lambda i: (i, 0))
    return pl.pallas_call(
        functools.partial(_peer_select_kernel, tn=tn, unroll=unroll),
        grid=(n // tn,),
        in_specs=[big, small, small, small],
        out_specs=big,
        out_shape=jax.ShapeDtypeStruct(s.shape, BF16),
        scratch_shapes=[pltpu.VMEM((tn, PEER_KEYS, PEER_KEYS), F32)] * 2,
        compiler_params=_cparams(("parallel",)),
        name="peer_select",
    )(s, ia, ib, gate)


def _peer_down_kernel(a_ref, w_ref, x_ref, g_ref, o_ref, acc_scr, *, final_norm):
    k = pl.program_id(1)

    @pl.when(k == 0)
    def _():
        acc_scr[...] = jnp.zeros(acc_scr.shape, F32)

    acc_scr[...] += jnp.dot(a_ref[...], w_ref[...], preferred_element_type=F32)

    @pl.when(k == pl.num_programs(1) - 1)
    def _():
        x = x_ref[...] + acc_scr[...]
        if final_norm:
            x = x * lax.rsqrt(jnp.mean(x * x, axis=-1, keepdims=True) + EPS) * g_ref[...]
        o_ref[...] = x


def _peer_down(a, w_down_bf, x1, norm_g, final_norm, tm=1024, tk=1024):
    n, e = a.shape
    d = w_down_bf.shape[1]
    return pl.pallas_call(
        functools.partial(_peer_down_kernel, final_norm=final_norm),
        grid=(n // tm, e // tk),
        in_specs=[pl.BlockSpec((tm, tk), lambda i, k: (i, k)), pl.BlockSpec((tk, d), lambda i, k: (k, 0)),
                  pl.BlockSpec((tm, d), lambda i, k: (i, 0)), pl.BlockSpec((1, d), lambda i, k: (0, 0))],
        out_specs=pl.BlockSpec((tm, d), lambda i, k: (i, 0)),
        out_shape=jax.ShapeDtypeStruct((n, d), F32),
        scratch_shapes=[pltpu.VMEM((tm, d), F32)],
        compiler_params=_cparams(("parallel", "arbitrary")),
        name="peer_down",
    )(a, w_down_bf, x1, norm_g.reshape(1, d))


def _peer(x1, h2, h2b, w_query, subkeys, w_up, w_down, norm_g, final_norm):
    n = x1.shape[0]
    ia, ib, gate = _peer_topk(h2, w_query, subkeys)
    s = _peer_up(h2b, w_up.astype(BF16))
    a = _peer_select(s, ia, ib, gate)
    return _peer_down(a, w_down.astype(BF16), x1, norm_g, final_norm)


def kernel(x_prompt, x_sample, state_delta, state_conv, cache_win0, cache_win1, cache_win2, state_pool, norm_mix, w_in, conv_w, a_log, dt_bias, dn_norm, w_pool, pool_scale, w_branch, w_out, norm_ffn, peer_query, peer_subkeys, peer_up, peer_down, norm_final):
    Bp, Tp, D = x_prompt.shape
    Bs, Ts, _ = x_sample.shape
    depth = w_in.shape[0]
    past = PAST_LEN
    npr = Bp * Tp
    caches = (cache_win0, cache_win1, cache_win2)
    caches_t = [jnp.transpose(c, (0, 1, 3, 4, 5, 2)) for c in caches]
    x = jnp.concatenate([x_prompt.reshape(npr, D), x_sample.reshape(Bs * Ts, D)], axis=0)
    cos_p, sin_p = _rope_tables(jnp.arange(Tp))
    cos_s, sin_s = _rope_tables(past + jnp.arange(Ts))
    nconv = CONV_W - 1
    dn_w = DN_HEADS * DN_DIM

    delta_p, delta_s, conv_p, conv_s, pool_p, pool_s = [], [], [], [], [], []
    win_p = [[] for _ in SWA_GROUPS]
    new_s = [[] for _ in SWA_GROUPS]
    for l in range(depth):
        P = _inproj(x, norm_mix[l], _pack_w_in(w_in[l]))
        Ps = P[npr:].reshape(Bs, Ts, P_COLS)
        cw8 = jnp.pad(conv_w[l], ((0, 8 - CONV_W), (0, 0)))

        def prompt_tail(rows, col, width):
            return jnp.stack([lax.slice(P, ((b + 1) * Tp - rows, col), ((b + 1) * Tp, col + width)) for b in range(Bp)])

        abT_p = jnp.transpose(lax.slice(P, (0, COL_AB), (npr, COL_AB + 8)).reshape(Bp, Tp, 8), (0, 2, 1))
        abT_s = jnp.transpose(Ps[:, :, COL_AB:COL_AB + 8], (0, 2, 1))
        oa_p, dp = _deltanet4(P, abT_p, jnp.zeros((Bp, 8, 3 * dn_w), F32), cw8,
                              jnp.zeros((Bp, DN_HEADS, DN_DIM, DN_DIM), F32), a_log[l], dt_bias[l], dn_norm[l],
                              0, Bp, Tp, 2 * DN_CHUNK, DN_CHUNK)
        oa_s, ds = _deltanet4(P, abT_s, jnp.pad(state_conv[l], ((0, 0), (8 - nconv, 0), (0, 0))), cw8,
                              state_delta[l], a_log[l], dt_bias[l], dn_norm[l], npr, Bs, Ts, Ts, min(DN_CHUNK, Ts))
        delta_p.append(dp)
        delta_s.append(ds)
        conv_p.append(prompt_tail(nconv, COL_QKV_A, 3 * dn_w))
        conv_s.append(jnp.concatenate([state_conv[l], Ps[:, :, COL_QKV_A:COL_QKV_A + 3 * dn_w]], axis=1)[:, -nconv:])

        attn_p, attn_s = [], []
        for gi, (window, dil) in enumerate(SWA_GROUPS):
            o_p, m_p, l_p, kr_p = _attn_prompt(P, cos_p, sin_p, gi, Bp, Tp)
            o_s, m_s, l_s, kr_s = _attn_sample(P, cos_s, sin_s, caches_t[gi], l, gi, npr, Bs, Ts)
            attn_p += [o_p, m_p, l_p]
            attn_s += [o_s, m_s, l_s]
            vcol = _qkvb_col(gi, 2)
            keep = min(window, Tp)
            kv_p = jnp.stack([kr_p.reshape(Bp, Tp, SWA_OUT)[:, Tp - keep:], prompt_tail(keep, vcol, SWA_OUT)], axis=2)
            win_p[gi].append(kv_p.reshape(Bp, keep, 2, SWA_HEADS, SWA_DIM))
            kv_s = jnp.stack([kr_s.reshape(Bs, Ts, SWA_OUT), Ps[:, :, vcol:vcol + SWA_OUT]], axis=2)
            new_s[gi].append(kv_s.reshape(Bs, Ts, 2, SWA_HEADS, SWA_DIM))

        wp_bf = w_pool[l].astype(BF16)
        oc_p = _pool(P, jnp.zeros((Bp, POOL_BUF + 1, POOL_WIDTH), F32), wp_bf, pool_scale[l], 0, Bp, Tp, 256, 0)
        oc_s = _pool(P, jnp.pad(state_pool[l], ((0, 0), (1, 0), (0, 0))), wp_bf, pool_scale[l], npr, Bs, Ts, Ts, past)
        pool_p.append(prompt_tail(POOL_BUF, COL_U, POOL_WIDTH))
        pool_s.append(jnp.concatenate([state_pool[l], Ps[:, :, COL_U:COL_U + POOL_WIDTH]], axis=1)[:, -POOL_BUF:])

        wb = w_branch[l].astype(BF16)
        x1, h2, h2b = _merge(x, P, [oa_p] + attn_p + [oc_p], [oa_s] + attn_s + [oc_s],
                             wb[:dn_w], wb[dn_w:dn_w + SWA_OUT], wb[dn_w + SWA_OUT:], w_out[l].astype(BF16), norm_ffn[l])
        x = _peer(x1, h2, h2b, peer_query[l], peer_subkeys[l], peer_up[l], peer_down[l], norm_final, l == depth - 1)

    y_p = x[:npr].reshape(Bp, Tp, D)
    y_s = x[npr:].reshape(Bs, Ts, D)
    st = jnp.stack
    win_s = []
    for gi, (window, dil) in enumerate(SWA_GROUPS):
        kv_all = jnp.concatenate([caches[gi], st(new_s[gi])], axis=2)
        win_s.append(kv_all[:, :, kv_all.shape[2] - min(window, kv_all.shape[2]):])
    return (y_p, y_s, st(delta_p), st(delta_s), st(conv_p), st(conv_s),
            st(win_p[0]), win_s[0], st(win_p[1]), win_s[1], st(win_p[2]), win_s[2], st(pool_p), st(pool_s))
```

```python
import functools

import jax
import jax.numpy as jnp
import numpy as np
from jax import lax
from jax.experimental import pallas as pl
from jax.experimental.pallas import tpu as pltpu

F32 = jnp.float32
BF16 = jnp.bfloat16
HI = lax.Precision.HIGHEST
EPS = 1e-6
NEG_INF = float("-inf")

D_MODEL = 1024
DN_HEADS = 4
DN_DIM = 128
DN_CHUNK = 64
CONV_W = 4
SWA_GROUPS = ((128, 1), (512, 4), (2048, 16))
SWA_HEADS = 4
SWA_DIM = 64
SWA_OUT = SWA_HEADS * SWA_DIM
SWA_BLOCK = 128
ROPE_THETA = 10000.0
POOL_WINDOWS = (2, 4, 8, 16)
POOL_GROUP = 384
POOL_WIDTH = 1536
POOL_BUF = 15
PEER_KEYS = 128
PEER_HEADS = 8
PEER_TOPK = 16
PEER_SEL = PEER_HEADS * PEER_TOPK
PAST_LEN = 2048

COL_QKV_A = 0
COL_U = 1536
COL_GATE = 3072
COL_Z = 6144
COL_QKV_B = 6656
COL_AB = 8960
P_COLS = 9216
VMEM_LIMIT = 56 * 1024 * 1024


def _cparams(sem):
    return pltpu.CompilerParams(dimension_semantics=sem, vmem_limit_bytes=VMEM_LIMIT)


def _sigmoid(x):
    return 1.0 / (1.0 + jnp.exp(-x))


def _silu(x):
    return x * _sigmoid(x)


def _inproj_kernel(x_ref, g_ref, w_ref, o_ref, h_scr):
    @pl.when(pl.program_id(1) == 0)
    def _():
        x = x_ref[...]
        y = x * lax.rsqrt(jnp.mean(x * x, axis=-1, keepdims=True) + EPS)
        h_scr[...] = (y * g_ref[...]).astype(BF16)

    o_ref[...] = jnp.dot(h_scr[...], w_ref[...], preferred_element_type=F32)


def _inproj(x, g, w_bf, tm=1024, tn=1024):
    n, d = x.shape
    nc = w_bf.shape[1]
    return pl.pallas_call(
        _inproj_kernel,
        grid=(n // tm, nc // tn),
        in_specs=[pl.BlockSpec((tm, d), lambda i, j: (i, 0)),
                  pl.BlockSpec((1, d), lambda i, j: (0, 0)),
                  pl.BlockSpec((d, tn), lambda i, j: (0, j))],
        out_specs=pl.BlockSpec((tm, tn), lambda i, j: (i, j)),
        out_shape=jax.ShapeDtypeStruct((n, nc), F32),
        scratch_shapes=[pltpu.VMEM((tm, d), BF16)],
        compiler_params=_cparams(("parallel", "arbitrary")),
        name="inproj",
    )(x, g.reshape(1, d), w_bf)


def _pack_w_in(w):
    d = w.shape[0]
    qkv_b = w[:, 2056:4360].reshape(d, 3, len(SWA_GROUPS), SWA_OUT).transpose(0, 2, 1, 3).reshape(d, 2304)
    parts = [w[:, 0:1536], w[:, 4360:5896], w[:, 5896:8968], w[:, 1536:2048], qkv_b, w[:, 2048:2056],
             jnp.zeros((d, P_COLS - 8968), w.dtype)]
    return jnp.concatenate(parts, axis=1).astype(BF16)


def _qkvb_col(gi, which):
    return COL_QKV_B + (gi * 3 + which) * SWA_OUT


def _dn_kernel(alog_ref, dtb_ref, q_ref, k_ref, v_ref, z_ref, ab_ref, abT_ref, c0q_ref, c0k_ref, c0v_ref,
               cwq_ref, cwk_ref, cwv_ref, s0_ref, dnn_ref, o_ref, sout_ref, S_scr, ext_scr, *, C, Tc):
    h = pl.program_id(1)
    t = pl.program_id(2)

    @pl.when(t == 0)
    def _():
        S_scr[...] = s0_ref[0, 0]
        ext_scr[0, 0:8, :] = c0q_ref[0]
        ext_scr[1, 0:8, :] = c0k_ref[0]
        ext_scr[2, 0:8, :] = c0v_ref[0]

    def conv_act(i, x_ref, cw_ref):
        x = x_ref[...]
        ext_scr[i, 8:8 + Tc, :] = x
        w = cw_ref[...]
        acc = (ext_scr[i, 5:5 + Tc, :] * w[0:1] + ext_scr[i, 6:6 + Tc, :] * w[1:2]
               + ext_scr[i, 7:7 + Tc, :] * w[2:3] + x * w[3:4])
        ext_scr[i, 0:8, :] = ext_scr[i, Tc:Tc + 8, :]
        return _silu(acc)

    def l2n(x):
        return x * lax.rsqrt(jnp.sum(x * x, axis=-1, keepdims=True) + EPS)

    q_all = l2n(conv_act(0, q_ref, cwq_ref)) * (DN_DIM ** -0.5)
    k_all = l2n(conv_act(1, k_ref, cwk_ref))
    v_all = conv_act(2, v_ref, cwv_ref)

    ab = ab_ref[...]
    lane = lax.broadcasted_iota(jnp.int32, ab.shape, 1)
    a_col = jnp.sum(jnp.where(lane == h, ab, 0.0), axis=1, keepdims=True)
    b_col = jnp.sum(jnp.where(lane == h + DN_HEADS, ab, 0.0), axis=1, keepdims=True)
    abT = abT_ref[0]
    sub = lax.broadcasted_iota(jnp.int32, abT.shape, 0)
    a_row = jnp.sum(jnp.where(sub == h, abT, 0.0), axis=0, keepdims=True)

    neg_a = -jnp.exp(jnp.full((1, 1), alog_ref[h], F32))
    dtb = dtb_ref[h]

    def softplus(x):
        return jnp.maximum(x, 0.0) + jnp.log1p(jnp.exp(-jnp.abs(x)))

    g_col_all = neg_a * softplus(a_col + dtb)
    g_row_all = neg_a * softplus(a_row + dtb)
    beta_all = _sigmoid(b_col)
    zs = _silu(z_ref[...])
    dnn = dnn_ref[...]

    ii = lax.broadcasted_iota(jnp.int32, (C, C), 0)
    jj = lax.broadcasted_iota(jnp.int32, (C, C), 1)
    incl = ii >= jj
    strict = ii > jj
    eye = jnp.where(ii == jj, 1.0, 0.0).astype(F32)

    def mm(a, b):
        return jnp.dot(a, b, precision=HI, preferred_element_type=F32)

    def mm_nt(a, b):
        return lax.dot_general(a, b, (((1,), (1,)), ((), ())), precision=HI, preferred_element_type=F32)

    def mm_tn(a, b):
        return lax.dot_general(a, b, (((0,), (0,)), ((), ())), precision=HI, preferred_element_type=F32)

    S = S_scr[...]
    for c in range(Tc // C):
        sl = slice(c * C, (c + 1) * C)
        qc, kc, vc = q_all[sl], k_all[sl], v_all[sl]
        g_col, g_row, beta = g_col_all[sl], g_row_all[:, sl], beta_all[sl]
        gc_col = jnp.sum(jnp.where(incl, g_row, 0.0), axis=1, keepdims=True)
        gc_row = jnp.sum(jnp.where(ii <= jj, g_col, 0.0), axis=0, keepdims=True)
        gc_last = jnp.sum(g_row, axis=1, keepdims=True)
        decay = jnp.where(incl, jnp.exp(jnp.where(incl, gc_col - gc_row, 0.0)), 0.0)
        kb = kc * beta
        vb = vc * beta
        egc = jnp.exp(gc_col)
        lower = mm_nt(kb, kc) * jnp.where(strict, decay, 0.0)
        pw = -lower
        inv = eye + pw
        m = 2
        while m < C:
            pw = mm(pw, pw)
            inv = inv + mm(inv, pw)
            m *= 2
        u = mm(inv, vb)
        w = mm(inv, kb * egc)
        intra = mm_nt(qc, kc) * decay
        q_dec = qc * egc
        k_dec = kc * jnp.exp(gc_last - gc_col)
        v_new = u - mm(w, S)
        o = mm(q_dec, S) + mm(intra, v_new)
        S = S * jnp.exp(gc_last) + mm_tn(k_dec, v_new)
        o = o * lax.rsqrt(jnp.mean(o * o, axis=-1, keepdims=True) + EPS) * dnn
        o_ref[sl, :] = o * zs[sl]
    S_scr[...] = S

    @pl.when(t == pl.num_programs(2) - 1)
    def _():
        sout_ref[0, 0] = S


def _deltanet(P, abT, conv0p, cw8, state0, a_log, dt_bias, dn_norm, row0, B, T, Tc, C):
    n = P.shape[0]
    nt = T // Tc
    rb0 = row0 // Tc
    H = DN_HEADS

    def rows(col0):
        return pl.BlockSpec((Tc, 128), lambda b, h, t, c=col0: (rb0 + b * nt + t, c + h))

    def c0(col0):
        return pl.BlockSpec((1, 8, 128), lambda b, h, t, c=col0: (b, 0, c + h))

    def cw(col0):
        return pl.BlockSpec((8, 128), lambda b, h, t, c=col0: (0, c + h))

    smem = pl.BlockSpec(memory_space=pltpu.SMEM)
    o, s_new = pl.pallas_call(
        functools.partial(_dn_kernel, C=C, Tc=Tc),
        grid=(B, H, nt),
        in_specs=[smem, smem, rows(0), rows(4), rows(8), rows(COL_Z // 128),
                  pl.BlockSpec((Tc, 128), lambda b, h, t: (rb0 + b * nt + t, COL_AB // 128)),
                  pl.BlockSpec((1, 8, Tc), lambda b, h, t: (b, 0, t)),
                  c0(0), c0(4), c0(8), cw(0), cw(4), cw(8),
                  pl.BlockSpec((1, 1, 128, 128), lambda b, h, t: (b, h, 0, 0)),
                  pl.BlockSpec((1, 128), lambda b, h, t: (0, 0))],
        out_specs=[pl.BlockSpec((Tc, 128), lambda b, h, t: (b * nt + t, h)),
                   pl.BlockSpec((1, 1, 128, 128), lambda b, h, t: (b, h, 0, 0))],
        out_shape=[jax.ShapeDtypeStruct((B * T, H * 128), F32),
                   jax.ShapeDtypeStruct((B, H, 128, 128), F32)],
        scratch_shapes=[pltpu.VMEM((128, 128), F32), pltpu.VMEM((3, Tc + 8, 128), F32)],
        compiler_params=_cparams(("parallel", "parallel", "arbitrary")),
        name="deltanet",
    )(a_log, dt_bias, P, P, P, P, P, abT, conv0p, conv0p, conv0p, cw8, cw8, cw8, state0, dn_norm.reshape(1, 128))
    return o, s_new


def _dn4_kernel(alog_ref, dtb_ref, q_ref, k_ref, v_ref, z_ref, ab_ref, abT_ref, c0_ref, cw_ref, s0_ref, dnn_ref,
                o_ref, sout_ref, S_scr, ext_scr, *, C, Tc):
    H, DH = DN_HEADS, DN_DIM
    W = H * DH
    t = pl.program_id(1)

    @pl.when(t == 0)
    def _():
        S_scr[...] = s0_ref[0]
        ext_scr[0:8, :] = c0_ref[0]

    ext_scr[8:8 + Tc, 0:W] = q_ref[...]
    ext_scr[8:8 + Tc, W:2 * W] = k_ref[...]
    ext_scr[8:8 + Tc, 2 * W:3 * W] = v_ref[...]
    cw = cw_ref[...]
    act = _silu(ext_scr[5:5 + Tc, :] * cw[0:1] + ext_scr[6:6 + Tc, :] * cw[1:2]
                + ext_scr[7:7 + Tc, :] * cw[2:3] + ext_scr[8:8 + Tc, :] * cw[3:4])
    ext_scr[0:8, :] = ext_scr[Tc:Tc + 8, :]

    def l2n(x):
        return x * lax.rsqrt(jnp.sum(x * x, axis=-1, keepdims=True) + EPS)

    def softplus(x):
        return jnp.maximum(x, 0.0) + jnp.log1p(jnp.exp(-jnp.abs(x)))

    def split(a):
        hi = a.astype(BF16)
        return hi, (a - hi.astype(F32)).astype(BF16)

    def mm3(a, b, dims):
        (ah, al), (bh, bl) = split(a), split(b)

        def dg(x, y):
            return lax.dot_general(x, y, (dims, ((), ())), preferred_element_type=F32)

        return dg(ah, bh) + dg(ah, bl) + dg(al, bh)

    def mm(a, b):
        return mm3(a, b, ((1,), (0,)))

    def mm_nt(a, b):
        return mm3(a, b, ((1,), (1,)))

    def mm_tn(a, b):
        return mm3(a, b, ((0,), (0,)))

    ab = ab_ref[...]
    abT = abT_ref[0]
    zs = _silu(z_ref[...])
    dnn = dnn_ref[...]
    ii = lax.broadcasted_iota(jnp.int32, (C, C), 0)
    jj = lax.broadcasted_iota(jnp.int32, (C, C), 1)
    incl = ii >= jj
    strict = ii > jj
    eye = jnp.where(ii == jj, 1.0, 0.0).astype(F32)
    nck = Tc // C

    pairs = []
    for h in range(H):
        qh = l2n(act[:, h * DH:(h + 1) * DH]) * (DH ** -0.5)
        kh = l2n(act[:, W + h * DH:W + (h + 1) * DH])
        vh = act[:, 2 * W + h * DH:2 * W + (h + 1) * DH]
        neg_a = -jnp.exp(jnp.full((1, 1), alog_ref[h], F32))
        g_col_h = neg_a * softplus(ab[:, h:h + 1] + dtb_ref[h])
        g_row_h = neg_a * softplus(abT[h:h + 1, :] + dtb_ref[h])
        beta_h = _sigmoid(ab[:, H + h:H + h + 1])
        for c in range(nck):
            sl = slice(c * C, (c + 1) * C)
            qc, kc, vc = qh[sl], kh[sl], vh[sl]
            g_col, g_row, beta = g_col_h[sl], g_row_h[:, sl], beta_h[sl]
            gc_col = jnp.sum(jnp.where(incl, g_row, 0.0), axis=1, keepdims=True)
            gc_row = jnp.sum(jnp.where(ii <= jj, g_col, 0.0), axis=0, keepdims=True)
            gc_last = jnp.sum(g_row, axis=1, keepdims=True)
            decay = jnp.where(incl, jnp.exp(jnp.where(incl, gc_col - gc_row, 0.0)), 0.0)
            kb = kc * beta
            egc = jnp.exp(gc_col)
            pairs.append(dict(h=h, c=c, sl=sl, qc=qc, kc=kc, kb=kb, vb=vc * beta, decay=decay, egc=egc,
                              kdec=kc * jnp.exp(gc_last - gc_col), glast=jnp.exp(gc_last)))
    for p in pairs:
        p["pw"] = -(mm_nt(p["kb"], p["kc"]) * jnp.where(strict, p["decay"], 0.0))
        p["intra"] = mm_nt(p["qc"], p["kc"]) * p["decay"]
    for p in pairs:
        p["inv"] = eye + p["pw"]
    m = 2
    while m < C:
        for p in pairs:
            p["pw"] = mm(p["pw"], p["pw"])
        for p in pairs:
            p["inv"] = p["inv"] + mm(p["inv"], p["pw"])
        m *= 2
    for p in pairs:
        p["u"] = mm(p["inv"], p["vb"])
        p["w"] = mm(p["inv"], p["kb"] * p["egc"])
    S = [S_scr[h] for h in range(H)]
    for c in range(nck):
        cur = [p for p in pairs if p["c"] == c]
        vnew = [p["u"] - mm(p["w"], S[p["h"]]) for p in cur]
        for p, vn in zip(cur, vnew):
            h = p["h"]
            o = mm(p["qc"] * p["egc"], S[h]) + mm(p["intra"], vn)
            S[h] = S[h] * p["glast"] + mm_tn(p["kdec"], vn)
            o = o * lax.rsqrt(jnp.mean(o * o, axis=-1, keepdims=True) + EPS) * dnn
            o_ref[p["sl"], h * DH:(h + 1) * DH] = o * zs[p["sl"], h * DH:(h + 1) * DH]
    for h in range(H):
        S_scr[h] = S[h]

    @pl.when(t == pl.num_programs(1) - 1)
    def _():
        for h in range(H):
            sout_ref[0, h] = S[h]


def _deltanet4(P, abT, conv0p, cw8, state0, a_log, dt_bias, dn_norm, row0, B, T, Tc, C):
    nt = T // Tc
    rb0 = row0 // Tc
    H, W = DN_HEADS, DN_HEADS * DN_DIM

    def rows(width, col):
        return pl.BlockSpec((Tc, width), lambda b, t, c=col: (rb0 + b * nt + t, c))

    smem = pl.BlockSpec(memory_space=pltpu.SMEM)
    return pl.pallas_call(
        functools.partial(_dn4_kernel, C=C, Tc=Tc),
        grid=(B, nt),
        in_specs=[smem, smem, rows(W, 0), rows(W, 1), rows(W, 2), rows(W, COL_Z // W), rows(128, COL_AB // 128),
                  pl.BlockSpec((1, 8, Tc), lambda b, t: (b, 0, t)),
                  pl.BlockSpec((1, 8, 3 * W), lambda b, t: (b, 0, 0)),
                  pl.BlockSpec((8, 3 * W), lambda b, t: (0, 0)),
                  pl.BlockSpec((1, H, DN_DIM, DN_DIM), lambda b, t: (b, 0, 0, 0)),
                  pl.BlockSpec((1, DN_DIM), lambda b, t: (0, 0))],
        out_specs=[pl.BlockSpec((Tc, W), lambda b, t: (b * nt + t, 0)),
                   pl.BlockSpec((1, H, DN_DIM, DN_DIM), lambda b, t: (b, 0, 0, 0))],
        out_shape=[jax.ShapeDtypeStruct((B * T, W), F32), jax.ShapeDtypeStruct((B, H, DN_DIM, DN_DIM), F32)],
        scratch_shapes=[pltpu.VMEM((H, DN_DIM, DN_DIM), F32), pltpu.VMEM((Tc + 8, 3 * W), F32)],
        compiler_params=_cparams(("parallel", "arbitrary")),
        name="deltanet",
    )(a_log, dt_bias, P, P, P, P, P, abT, conv0p, cw8, state0, dn_norm.reshape(1, DN_DIM))


def _rope_tables(pos):
    half = SWA_DIM // 2
    inv = ROPE_THETA ** (-jnp.arange(half, dtype=F32) / half)
    ang = pos.astype(F32)[:, None] * inv[None, :]
    cos = jnp.tile(jnp.cos(ang), (1, 2 * SWA_HEADS))
    sin = jnp.tile(jnp.sin(ang), (1, 2 * SWA_HEADS))
    return cos, sin


def _rope(x, cos, sin):
    lane = lax.broadcasted_iota(jnp.int32, x.shape, 1)
    first = (lane % SWA_DIM) < (SWA_DIM // 2)
    w = x.shape[1]
    xr = jnp.where(first, -pltpu.roll(x, w - SWA_DIM // 2, 1), pltpu.roll(x, SWA_DIM // 2, 1))
    return x * cos + xr * sin


def _mm_nt(a, b):
    return lax.dot_general(a, b, (((1,), (1,)), ((), ())), precision=HI, preferred_element_type=F32)


def _mm(a, b):
    return jnp.dot(a, b, precision=HI, preferred_element_type=F32)


def _attn_prompt_kernel(q_ref, kc_ref, kp_ref, vc_ref, vp_ref, cq_ref, sq_ref, cp_ref, sp_ref,
                        o_ref, m_ref, l_ref, kr_ref, *, n_back):
    i = pl.program_id(2)
    Bq = SWA_BLOCK
    q = _rope(q_ref[...], cq_ref[...], sq_ref[...])
    k_cur = _rope(kc_ref[...], cq_ref[...], sq_ref[...])
    k_prev = _rope(kp_ref[...], cp_ref[...], sp_ref[...])
    kr_ref[...] = k_cur
    kcat = jnp.concatenate([k_prev, k_cur], axis=0)
    vcat = jnp.concatenate([vp_ref[...], vc_ref[...]], axis=0)
    H = SWA_HEADS
    qi = lax.broadcasted_iota(jnp.int32, (H * Bq, 2 * Bq), 0) % Bq
    kj = lax.broadcasted_iota(jnp.int32, (H * Bq, 2 * Bq), 1)
    rel = Bq + qi - kj
    valid = (rel >= 0) & (rel <= n_back) & (i * Bq + qi - rel >= 0)
    head = lax.broadcasted_iota(jnp.int32, (Bq, SWA_OUT), 1) // SWA_DIM
    qs = jnp.concatenate([jnp.where(head == h, q, 0.0) for h in range(H)], axis=0).astype(BF16)
    s = lax.dot_general(qs, kcat.astype(BF16), (((1,), (1,)), ((), ())), preferred_element_type=F32)
    s = jnp.where(valid, s * (SWA_DIM ** -0.5), NEG_INF)
    m = jnp.max(s, axis=1, keepdims=True)
    p = jnp.exp(s - m)
    l = jnp.sum(p, axis=1, keepdims=True)
    acc = jnp.dot(p.astype(BF16), vcat.astype(BF16), preferred_element_type=F32)
    o = jnp.zeros((Bq, SWA_OUT), F32)
    m_full = jnp.zeros((Bq, SWA_OUT), F32)
    l_full = jnp.zeros((Bq, SWA_OUT), F32)
    for h in range(H):
        rows = slice(h * Bq, (h + 1) * Bq)
        o = jnp.where(head == h, acc[rows], o)
        m_full = jnp.where(head == h, m[rows], m_full)
        l_full = jnp.where(head == h, l[rows], l_full)
    o_ref[...] = o
    m_ref[...] = m_full
    l_ref[...] = l_full


def _attn_prompt(P, cos, sin, gi, B, T):
    window, dil = SWA_GROUPS[gi]
    n_back = window // dil
    assert n_back <= SWA_BLOCK
    L = T // dil
    nb = L // SWA_BLOCK
    if dil == 1:
        Pd, cw, c0 = P, 0, _qkvb_col(gi, 0) // SWA_OUT
    else:
        Pd = P[:B * T, _qkvb_col(gi, 0):_qkvb_col(gi, 0) + 3 * SWA_OUT].reshape(B * L, dil * 3 * SWA_OUT)
        cw, c0 = 3, 0
    cd = cos.reshape(L, dil * SWA_OUT)
    sd = sin.reshape(L, dil * SWA_OUT)
    blk = (SWA_BLOCK, SWA_OUT)

    def cur(which):
        return pl.BlockSpec(blk, lambda b, p, i, c=c0 + which: (b * nb + i, p * cw + c))

    def prev(which):
        return pl.BlockSpec(blk, lambda b, p, i, c=c0 + which: (b * nb + jnp.maximum(i - 1, 0), p * cw + c))

    tcur = pl.BlockSpec(blk, lambda b, p, i: (i, p))
    tprev = pl.BlockSpec(blk, lambda b, p, i: (jnp.maximum(i - 1, 0), p))
    ospec = pl.BlockSpec(blk, lambda b, p, i: (b * nb + i, p))
    oshape = jax.ShapeDtypeStruct((B * L, dil * SWA_OUT), F32)
    outs = pl.pallas_call(
        functools.partial(_attn_prompt_kernel, n_back=n_back),
        grid=(B, dil, nb),
        in_specs=[cur(0), cur(1), prev(1), cur(2), prev(2), tcur, tcur, tprev, tprev],
        out_specs=[ospec] * 4,
        out_shape=[oshape] * 4,
        compiler_params=_cparams(("parallel", "parallel", "arbitrary")),
        name=f"attn_prompt{gi}",
    )(Pd, Pd, Pd, Pd, Pd, cd, sd, cd, sd)
    return [a.reshape(B * T, SWA_OUT) for a in outs]


def _attn_sample_kernel(q_ref, k_ref, v_ref, cos_ref, sin_ref, kv_ref, o_ref, m_ref, l_ref, kr_ref, *, dil):
    T = q_ref.shape[0]
    W = kv_ref.shape[-1]
    H = SWA_HEADS
    q = _rope(q_ref[...], cos_ref[...], sin_ref[...])
    k_new = _rope(k_ref[...], cos_ref[...], sin_ref[...])
    v_new = v_ref[...]
    kr_ref[...] = k_new
    head = lax.broadcasted_iota(jnp.int32, (T, SWA_OUT), 1) // SWA_DIM
    tq = lax.broadcasted_iota(jnp.int32, (H * T, W), 0) % T
    wk = lax.broadcasted_iota(jnp.int32, (H * T, W), 1)
    valid_buf = (((W + tq - wk) & (dil - 1)) == 0) & (wk >= tq)
    tq2 = lax.broadcasted_iota(jnp.int32, (H * T, T), 0) % T
    tk2 = lax.broadcasted_iota(jnp.int32, (H * T, T), 1)
    valid_new = (((tq2 - tk2) & (dil - 1)) == 0) & (tk2 <= tq2)
    scale = SWA_DIM ** -0.5
    nt_dims = (((1,), (1,)), ((), ()))
    qs = jnp.concatenate([jnp.where(head == h, q, 0.0) for h in range(H)], axis=0).astype(BF16)
    k_t = kv_ref[0, 0, 0].reshape(H * SWA_DIM, W).astype(BF16)
    v_t = kv_ref[0, 0, 1].reshape(H * SWA_DIM, W).astype(BF16)
    sn = lax.dot_general(qs, k_new.astype(BF16), nt_dims, preferred_element_type=F32)
    sn = jnp.where(valid_new, sn * scale, NEG_INF)
    sb = jnp.where(valid_buf, jnp.dot(qs, k_t, preferred_element_type=F32) * scale, NEG_INF)
    m = jnp.maximum(jnp.max(sn, axis=1, keepdims=True), jnp.max(sb, axis=1, keepdims=True))
    pn = jnp.exp(sn - m)
    pb = jnp.exp(sb - m)
    l = jnp.sum(pn, axis=1, keepdims=True) + jnp.sum(pb, axis=1, keepdims=True)
    acc = (jnp.dot(pn.astype(BF16), v_new.astype(BF16), preferred_element_type=F32)
           + lax.dot_general(pb.astype(BF16), v_t, nt_dims, preferred_element_type=F32))
    o = jnp.zeros((T, SWA_OUT), F32)
    m_full = jnp.zeros((T, SWA_OUT), F32)
    l_full = jnp.zeros((T, SWA_OUT), F32)
    for h in range(H):
        rows = slice(h * T, (h + 1) * T)
        o = jnp.where(head == h, acc[rows], o)
        m_full = jnp.where(head == h, m[rows], m_full)
        l_full = jnp.where(head == h, l[rows], l_full)
    o_ref[...] = o
    m_ref[...] = m_full
    l_ref[...] = l_full


def _attn_sample(P, cos, sin, cache_t, layer, gi, row0, B, T):
    window, dil = SWA_GROUPS[gi]
    W = cache_t.shape[-1]
    assert W == window and W % dil == 0 and dil & (dil - 1) == 0 and row0 % T == 0
    rb0 = row0 // T
    blk = (T, SWA_OUT)

    def rows(which):
        return pl.BlockSpec(blk, lambda b, c=_qkvb_col(gi, which) // SWA_OUT: (rb0 + b, c))

    tab = pl.BlockSpec(blk, lambda b: (0, 0))
    ospec = pl.BlockSpec(blk, lambda b: (b, 0))
    oshape = jax.ShapeDtypeStruct((B * T, SWA_OUT), F32)
    return pl.pallas_call(
        functools.partial(_attn_sample_kernel, dil=dil),
        grid=(B,),
        in_specs=[rows(0), rows(1), rows(2), tab, tab,
                  pl.BlockSpec((1, 1, 2, SWA_HEADS, SWA_DIM, W), lambda b: (layer, b, 0, 0, 0, 0))],
        out_specs=[ospec] * 4,
        out_shape=[oshape] * 4,
        compiler_params=_cparams(("parallel",)),
        name=f"attn_sample{gi}",
    )(P, P, P, cos, sin, cache_t)


def _pool_kernel(u_ref, pre_ref, w_ref, sc_ref, o_ref, ext_scr, *, Tc, pos0):
    t = pl.program_id(1)
    nb = POOL_BUF + 1

    @pl.when(t == 0)
    def _():
        ext_scr[0:nb, :] = pre_ref[0]

    u = u_ref[...]
    ext_scr[nb:nb + Tc, :] = u
    pos = (pos0 + t * Tc + lax.broadcasted_iota(jnp.int32, (Tc, 1), 0)).astype(F32)
    for gi, w in enumerate(POOL_WINDOWS):
        lo, hi = gi * POOL_GROUP, (gi + 1) * POOL_GROUP
        ug = u[:, lo:hi]
        acc = ug
        for i in range(1, w):
            acc = acc + ext_scr[nb - i:nb - i + Tc, lo:hi]
        d = acc / jnp.minimum(float(w), pos + 1.0) - ug
        y = jnp.dot(d.astype(BF16), w_ref[gi], preferred_element_type=F32)
        o_ref[:, lo:hi] = y * sc_ref[:, lo:hi]
    ext_scr[0:nb, :] = ext_scr[Tc:Tc + nb, :]


def _pool(P, prefix16, w_pool_bf, pool_scale, row0, B, T, Tc, pos0):
    nt = T // Tc
    rb0 = row0 // Tc
    return pl.pallas_call(
        functools.partial(_pool_kernel, Tc=Tc, pos0=pos0),
        grid=(B, nt),
        in_specs=[pl.BlockSpec((Tc, POOL_WIDTH), lambda b, t: (rb0 + b * nt + t, COL_U // POOL_WIDTH)),
                  pl.BlockSpec((1, POOL_BUF + 1, POOL_WIDTH), lambda b, t: (b, 0, 0)),
                  pl.BlockSpec((len(POOL_WINDOWS), POOL_GROUP, POOL_GROUP), lambda b, t: (0, 0, 0)),
                  pl.BlockSpec((1, POOL_WIDTH), lambda b, t: (0, 0))],
        out_specs=pl.BlockSpec((Tc, POOL_WIDTH), lambda b, t: (b * nt + t, 0)),
        out_shape=jax.ShapeDtypeStruct((B * T, POOL_WIDTH), F32),
        scratch_shapes=[pltpu.VMEM((Tc + POOL_BUF + 1, POOL_WIDTH), F32)],
        compiler_params=_cparams(("parallel", "arbitrary")),
        name="pool",
    )(P, prefix16, w_pool_bf, pool_scale.reshape(1, POOL_WIDTH))


N_MIX = 11


def _merge_kernel(x_ref, g0_ref, g1_ref, g2_ref, *refs, np_blocks):
    mix_p, mix_s = refs[:N_MIX], refs[N_MIX:2 * N_MIX]
    wa_ref, wb_ref, wc_ref, wo_ref, nf_ref, x1_ref, h2b_ref, h2lo_ref = refs[2 * N_MIX:]
    is_p = pl.program_id(0) < np_blocks
    oa, o0, m0, l0, o1, m1, l1, o2, m2, l2, oc = [jnp.where(is_p, p[...], s[...]) for p, s in zip(mix_p, mix_s)]
    mx = jnp.maximum(jnp.maximum(m0, m1), m2)
    e0, e1, e2 = jnp.exp(m0 - mx), jnp.exp(m1 - mx), jnp.exp(m2 - mx)
    ob = (e0 * o0 + e1 * o1 + e2 * o2) / (e0 * l0 + e1 * l1 + e2 * l2)

    def proj(a, w_ref):
        return jnp.dot(a.astype(BF16), w_ref[...], preferred_element_type=F32)

    merged = (_sigmoid(g0_ref[...]) * proj(oa, wa_ref) + _sigmoid(g1_ref[...]) * proj(ob, wb_ref)
              + _sigmoid(g2_ref[...]) * proj(oc, wc_ref))
    x1 = x_ref[...] + proj(merged, wo_ref)
    x1_ref[...] = x1
    h2 = x1 * lax.rsqrt(jnp.mean(x1 * x1, axis=-1, keepdims=True) + EPS) * nf_ref[...]
    h2b = h2.astype(BF16)
    h2b_ref[...] = h2b
    h2lo_ref[...] = (h2 - h2b.astype(F32)).astype(BF16)


def _merge(x, P, mix_p, mix_s, wa, wb, wc, wo, norm_ffn, tm=256):
    n, d = x.shape
    npb = mix_p[0].shape[0] // tm
    nsb = mix_s[0].shape[0] // tm
    assert npb + nsb == n // tm

    def rows(width, col=0):
        return pl.BlockSpec((tm, width), lambda i, c=col: (i, c))

    def prow(a):
        return pl.BlockSpec((tm, a.shape[1]), lambda i: (jnp.minimum(i, npb - 1), 0))

    def srow(a):
        return pl.BlockSpec((tm, a.shape[1]), lambda i: (jnp.maximum(i - npb, 0), 0))

    def full(a):
        return pl.BlockSpec(a.shape, lambda i: (0,) * a.ndim)

    nf = norm_ffn.reshape(1, d)
    g0 = COL_GATE // d
    return pl.pallas_call(
        functools.partial(_merge_kernel, np_blocks=npb),
        grid=(n // tm,),
        in_specs=[rows(d), rows(d, g0), rows(d, g0 + 1), rows(d, g0 + 2)] + [prow(a) for a in mix_p]
        + [srow(a) for a in mix_s] + [full(wa), full(wb), full(wc), full(wo), full(nf)],
        out_specs=[rows(d), rows(d), rows(d)],
        out_shape=[jax.ShapeDtypeStruct((n, d), F32), jax.ShapeDtypeStruct((n, d), BF16),
                   jax.ShapeDtypeStruct((n, d), BF16)],
        compiler_params=_cparams(("parallel",)),
        name="merge",
    )(x, P, P, P, *mix_p, *mix_s, wa, wb, wc, wo, nf)


def _topk_rows(s, iota, k):
    vals, idxs = [], []
    big = 1e9
    for _ in range(k):
        m = jnp.max(s, axis=0, keepdims=True)
        idx = jnp.min(jnp.where(s == m, iota, big), axis=0, keepdims=True)
        vals.append(m)
        idxs.append(idx)
        s = jnp.where(iota == idx, NEG_INF, s)
    return jnp.concatenate(vals, axis=0), jnp.concatenate(idxs, axis=0)


def _peer_topk_kernel(hh_ref, hl_ref, wqh_ref, wql_ref, sk_ref, ia_ref, ib_ref, g_ref, q_scr, ia_scr, ib_scr, g_scr,
                      *, tn):
    TK = PEER_TOPK
    CH = 128
    q_scr[...] = (jnp.dot(hh_ref[...], wqh_ref[...], preferred_element_type=F32)
                  + jnp.dot(hh_ref[...], wql_ref[...], preferred_element_type=F32)
                  + jnp.dot(hl_ref[...], wqh_ref[...], preferred_element_type=F32))
    kio = lax.broadcasted_iota(jnp.int32, (PEER_KEYS, CH), 0).astype(F32)
    rho = lax.broadcasted_iota(jnp.int32, (TK + 8 * 8, CH), 0)
    mid = rho - TK
    cand_r = jnp.where(rho < TK, 0, jnp.where(mid < 56, (mid >> 3) + 1, mid - 56 + 8))
    cand_c = jnp.where(rho < TK, rho, jnp.where(mid < 56, mid & 7, 0))
    cand_ok = (cand_r + 1) * (cand_c + 1) <= TK
    cio = (cand_r * TK + cand_c).astype(F32)

    def chunk(j, carry):
        r0 = pl.multiple_of(j * CH, CH)

        def head(h, carry2):
            c0 = pl.multiple_of(h * 2 * PEER_KEYS, 2 * PEER_KEYS)
            q1 = q_scr[pl.ds(r0, CH), pl.ds(c0, PEER_KEYS)]
            q2 = q_scr[pl.ds(r0, CH), pl.ds(c0 + PEER_KEYS, PEER_KEYS)]
            s1, i1 = _topk_rows(_mm_nt(sk_ref[0], q1), kio, TK)
            s2, i2 = _topk_rows(_mm_nt(sk_ref[1], q2), kio, TK)
            cand = jnp.concatenate([s1[0:1] + s2] + [s1[r:r + 1] + s2[0:8] for r in range(1, 8)]
                                   + [s1[8:TK] + s2[0:1]], axis=0)
            best, ic = _topk_rows(jnp.where(cand_ok, cand, NEG_INF), cio, TK)
            rk = jnp.floor(ic * (1.0 / TK))
            ck = ic - rk * TK
            ia = jnp.zeros((TK, CH), F32)
            ib = jnp.zeros((TK, CH), F32)
            for r in range(TK):
                ia = jnp.where(rk == float(r), i1[r:r + 1], ia)
                ib = jnp.where(ck == float(r), i2[r:r + 1], ib)
            e = jnp.exp(best - best[0:1])
            gate = e / jnp.sum(e, axis=0, keepdims=True)
            o0 = pl.multiple_of(h * TK, TK)
            ia_scr[pl.ds(o0, TK), :] = ia
            ib_scr[pl.ds(o0, TK), :] = ib
            g_scr[pl.ds(o0, TK), :] = gate
            return carry2

        lax.fori_loop(0, PEER_HEADS, head, 0, unroll=4)
        ia_ref[pl.ds(r0, CH), :] = ia_scr[...].T.astype(jnp.int32)
        ib_ref[pl.ds(r0, CH), :] = ib_scr[...].T.astype(jnp.int32)
        g_ref[pl.ds(r0, CH), :] = g_scr[...].T
        return carry

    lax.fori_loop(0, tn // 128, chunk, 0)


def _peer_topk(h_hi, h_lo, w_query, subkeys, tn=256):
    n, d = h_hi.shape
    nq = w_query.shape[1]
    wq_hi = w_query.astype(BF16)
    wq_lo = (w_query - wq_hi.astype(F32)).astype(BF16)
    out = pl.BlockSpec((tn, PEER_SEL), lambda i: (i, 0))
    return pl.pallas_call(
        functools.partial(_peer_topk_kernel, tn=tn),
        grid=(n // tn,),
        in_specs=[pl.BlockSpec((tn, d), lambda i: (i, 0)),
                  pl.BlockSpec((tn, d), lambda i: (i, 0)),
                  pl.BlockSpec((d, nq), lambda i: (0, 0)),
                  pl.BlockSpec((d, nq), lambda i: (0, 0)),
                  pl.BlockSpec(subkeys.shape, lambda i: (0, 0, 0))],
        out_specs=[out, out, out],
        out_shape=[jax.ShapeDtypeStruct((n, PEER_SEL), jnp.int32), jax.ShapeDtypeStruct((n, PEER_SEL), jnp.int32),
                   jax.ShapeDtypeStruct((n, PEER_SEL), F32)],
        scratch_shapes=[pltpu.VMEM((tn, nq), F32), pltpu.VMEM((PEER_SEL, 128), F32),
                        pltpu.VMEM((PEER_SEL, 128), F32), pltpu.VMEM((PEER_SEL, 128), F32)],
        compiler_params=_cparams(("parallel",)),
        name="peer_topk",
    )(h_hi, h_lo, wq_hi, wq_lo, subkeys)


def _peer_up_kernel(h_ref, w_ref, o_ref):
    o_ref[...] = lax.dot_general(h_ref[...], w_ref[...], (((1,), (1,)), ((), ())),
                                 preferred_element_type=F32).astype(o_ref.dtype)


def _peer_up(h2b, w_up_bf, tm=1024, tn=1024):
    n, d = h2b.shape
    e = w_up_bf.shape[0]
    return pl.pallas_call(
        _peer_up_kernel,
        grid=(n // tm, e // tn),
        in_specs=[pl.BlockSpec((tm, d), lambda i, j: (i, 0)), pl.BlockSpec((tn, d), lambda i, j: (j, 0))],
        out_specs=pl.BlockSpec((tm, tn), lambda i, j: (i, j)),
        out_shape=jax.ShapeDtypeStruct((n, e), BF16),
        compiler_params=_cparams(("parallel", "arbitrary")),
        name="peer_up",
    )(h2b, w_up_bf)


def _peer_select_kernel(s_ref, ia_ref, ib_ref, g_ref, a_ref, x_scr, y_scr, *, tn, unroll):
    NK = PEER_KEYS
    SUB = 8

    def unpack(a8, carry):
        col = pl.multiple_of(a8 * SUB * NK, SUB * NK)
        x = jnp.stack([s_ref[:, pl.ds(col + i * NK, NK)].astype(F32) for i in range(SUB)], axis=0)
        x_scr[:, pl.ds(pl.multiple_of(a8 * SUB, SUB), SUB), :] = jnp.swapaxes(x, 0, 1)
        return carry

    lax.fori_loop(0, NK // SUB, unpack, 0)
    sub = lax.broadcasted_iota(jnp.int32, (NK, PEER_SEL), 0)

    def group(gidx, carry):
        toks = [gidx * unroll + u for u in range(unroll)]
        is_a = [sub == ia_ref[pl.ds(t, 1), :] for t in toks]
        onehot_b = [jnp.where(sub == ib_ref[pl.ds(t, 1), :], 1.0, 0.0).astype(BF16) for t in toks]
        vals = []
        for u, t in enumerate(toks):
            r = jnp.dot(x_scr[t].astype(BF16), onehot_b[u], preferred_element_type=F32)
            vals.append(jnp.sum(jnp.where(is_a[u], r, 0.0), axis=0, keepdims=True))
        for u, t in enumerate(toks):
            v = vals[u]
            act = g_ref[pl.ds(t, 1), :] * (0.5 * v * (1.0 + lax.erf(v * (2.0 ** -0.5))))
            wa = jnp.where(is_a[u], act, 0.0).astype(BF16)
            y_scr[t] = lax.dot_general(wa, onehot_b[u], (((1,), (1,)), ((), ())), preferred_element_type=F32)
        return carry

    lax.fori_loop(0, tn // unroll, group, 0)

    def pack(a8, carry):
        col = pl.multiple_of(a8 * SUB * NK, SUB * NK)
        y = jnp.swapaxes(y_scr[:, pl.ds(pl.multiple_of(a8 * SUB, SUB), SUB), :], 0, 1)
        for i in range(SUB):
            a_ref[:, pl.ds(col + i * NK, NK)] = y[i].astype(a_ref.dtype)
        return carry

    lax.fori_loop(0, NK // SUB, pack, 0)


def _peer_select(s, ia, ib, gate, tn=128, unroll=16):
    n, e = s.shape
    big = pl.BlockSpec((tn, e), lambda i: (i, 0))
    small = pl.BlockSpec((tn, PEER_SEL), lambda i: (i, 0))
    return pl.pallas_call(
        functools.partial(_peer_select_kernel, tn=tn, unroll=unroll),
        grid=(n // tn,),
        in_specs=[big, small, small, small],
        out_specs=big,
        out_shape=jax.ShapeDtypeStruct(s.shape, BF16),
        scratch_shapes=[pltpu.VMEM((tn, PEER_KEYS, PEER_KEYS), F32)] * 2,
        compiler_params=_cparams(("parallel",)),
        name="peer_select",
    )(s, ia, ib, gate)


def _peer_down_kernel(a_ref, w_ref, x_ref, g_ref, o_ref, acc_scr, *, final_norm):
    k = pl.program_id(1)

    @pl.when(k == 0)
    def _():
        acc_scr[...] = jnp.zeros(acc_scr.shape, F32)

    acc_scr[...] += jnp.dot(a_ref[...], w_ref[...], preferred_element_type=F32)

    @pl.when(k == pl.num_programs(1) - 1)
    def _():
        x = x_ref[...] + acc_scr[...]
        if final_norm:
            x = x * lax.rsqrt(jnp.mean(x * x, axis=-1, keepdims=True) + EPS) * g_ref[...]
        o_ref[...] = x


def _peer_down(a, w_down_bf, x1, norm_g, final_norm, tm=1024, tk=1024):
    n, e = a.shape
    d = w_down_bf.shape[1]
    return pl.pallas_call(
        functools.partial(_peer_down_kernel, final_norm=final_norm),
        grid=(n // tm, e // tk),
        in_specs=[pl.BlockSpec((tm, tk), lambda i, k: (i, k)), pl.BlockSpec((tk, d), lambda i, k: (k, 0)),
                  pl.BlockSpec((tm, d), lambda i, k: (i, 0)), pl.BlockSpec((1, d), lambda i, k: (0, 0))],
        out_specs=pl.BlockSpec((tm, d), lambda i, k: (i, 0)),
        out_shape=jax.ShapeDtypeStruct((n, d), F32),
        scratch_shapes=[pltpu.VMEM((tm, d), F32)],
        compiler_params=_cparams(("parallel", "arbitrary")),
        name="peer_down",
    )(a, w_down_bf, x1, norm_g.reshape(1, d))


def _peer(x1, h2b, h2lo, w_query, subkeys, w_up, w_down, norm_g, final_norm):
    ia, ib, gate = _peer_topk(h2b, h2lo, w_query, subkeys)
    s = _peer_up(h2b, w_up.astype(BF16))
    a = _peer_select(s, ia, ib, gate)
    return _peer_down(a, w_down.astype(BF16), x1, norm_g, final_norm)


def kernel(x_prompt, x_sample, state_delta, state_conv, cache_win0, cache_win1, cache_win2, state_pool, norm_mix, w_in, conv_w, a_log, dt_bias, dn_norm, w_pool, pool_scale, w_branch, w_out, norm_ffn, peer_query, peer_subkeys, peer_up, peer_down, norm_final):
    Bp, Tp, D = x_prompt.shape
    Bs, Ts, _ = x_sample.shape
    depth = w_in.shape[0]
    past = PAST_LEN
    npr = Bp * Tp
    caches = (cache_win0, cache_win1, cache_win2)
    caches_t = [jnp.transpose(c, (0, 1, 3, 4, 5, 2)) for c in caches]
    x = jnp.concatenate([x_prompt.reshape(npr, D), x_sample.reshape(Bs * Ts, D)], axis=0)
    cos_p, sin_p = _rope_tables(jnp.arange(Tp))
    cos_s, sin_s = _rope_tables(past + jnp.arange(Ts))
    nconv = CONV_W - 1
    dn_w = DN_HEADS * DN_DIM

    delta_p, delta_s, conv_p, conv_s, pool_p, pool_s = [], [], [], [], [], []
    win_p = [[] for _ in SWA_GROUPS]
    new_s = [[] for _ in SWA_GROUPS]
    for l in range(depth):
        P = _inproj(x, norm_mix[l], _pack_w_in(w_in[l]))
        Ps = P[npr:].reshape(Bs, Ts, P_COLS)
        cw8 = jnp.pad(conv_w[l], ((0, 8 - CONV_W), (0, 0)))

        def prompt_tail(rows, col, width):
            return jnp.stack([lax.slice(P, ((b + 1) * Tp - rows, col), ((b + 1) * Tp, col + width)) for b in range(Bp)])

        abT_p = jnp.transpose(lax.slice(P, (0, COL_AB), (npr, COL_AB + 8)).reshape(Bp, Tp, 8), (0, 2, 1))
        abT_s = jnp.transpose(Ps[:, :, COL_AB:COL_AB + 8], (0, 2, 1))
        oa_p, dp = _deltanet4(P, abT_p, jnp.zeros((Bp, 8, 3 * dn_w), F32), cw8,
                              jnp.zeros((Bp, DN_HEADS, DN_DIM, DN_DIM), F32), a_log[l], dt_bias[l], dn_norm[l],
                              0, Bp, Tp, 2 * DN_CHUNK, DN_CHUNK)
        oa_s, ds = _deltanet4(P, abT_s, jnp.pad(state_conv[l], ((0, 0), (8 - nconv, 0), (0, 0))), cw8,
                              state_delta[l], a_log[l], dt_bias[l], dn_norm[l], npr, Bs, Ts, Ts, min(DN_CHUNK, Ts))
        delta_p.append(dp)
        delta_s.append(ds)
        conv_p.append(prompt_tail(nconv, COL_QKV_A, 3 * dn_w))
        conv_s.append(jnp.concatenate([state_conv[l], Ps[:, :, COL_QKV_A:COL_QKV_A + 3 * dn_w]], axis=1)[:, -nconv:])

        attn_p, attn_s = [], []
        for gi, (window, dil) in enumerate(SWA_GROUPS):
            o_p, m_p, l_p, kr_p = _attn_prompt(P, cos_p, sin_p, gi, Bp, Tp)
            o_s, m_s, l_s, kr_s = _attn_sample(P, cos_s, sin_s, caches_t[gi], l, gi, npr, Bs, Ts)
            attn_p += [o_p, m_p, l_p]
            attn_s += [o_s, m_s, l_s]
            vcol = _qkvb_col(gi, 2)
            keep = min(window, Tp)
            kv_p = jnp.stack([kr_p.reshape(Bp, Tp, SWA_OUT)[:, Tp - keep:], prompt_tail(keep, vcol, SWA_OUT)], axis=2)
            win_p[gi].append(kv_p.reshape(Bp, keep, 2, SWA_HEADS, SWA_DIM))
            kv_s = jnp.stack([kr_s.reshape(Bs, Ts, SWA_OUT), Ps[:, :, vcol:vcol + SWA_OUT]], axis=2)
            new_s[gi].append(kv_s.reshape(Bs, Ts, 2, SWA_HEADS, SWA_DIM))

        wp_bf = w_pool[l].astype(BF16)
        oc_p = _pool(P, jnp.zeros((Bp, POOL_BUF + 1, POOL_WIDTH), F32), wp_bf, pool_scale[l], 0, Bp, Tp, 256, 0)
        oc_s = _pool(P, jnp.pad(state_pool[l], ((0, 0), (1, 0), (0, 0))), wp_bf, pool_scale[l], npr, Bs, Ts, Ts, past)
        pool_p.append(prompt_tail(POOL_BUF, COL_U, POOL_WIDTH))
        pool_s.append(jnp.concatenate([state_pool[l], Ps[:, :, COL_U:COL_U + POOL_WIDTH]], axis=1)[:, -POOL_BUF:])

        wb = w_branch[l].astype(BF16)
        x1, h2b, h2lo = _merge(x, P, [oa_p] + attn_p + [oc_p], [oa_s] + attn_s + [oc_s],
                               wb[:dn_w], wb[dn_w:dn_w + SWA_OUT], wb[dn_w + SWA_OUT:], w_out[l].astype(BF16), norm_ffn[l])
        x = _peer(x1, h2b, h2lo, peer_query[l], peer_subkeys[l], peer_up[l], peer_down[l], norm_final, l == depth - 1)

    y_p = x[:npr].reshape(Bp, Tp, D)
    y_s = x[npr:].reshape(Bs, Ts, D)
    st = jnp.stack
    win_s = []
    for gi, (window, dil) in enumerate(SWA_GROUPS):
        kv_all = jnp.concatenate([caches[gi], st(new_s[gi])], axis=2)
        win_s.append(kv_all[:, :, kv_all.shape[2] - min(window, kv_all.shape[2]):])
    return (y_p, y_s, st(delta_p), st(delta_s), st(conv_p), st(conv_s),
            st(win_p[0]), win_s[0], st(win_p[1]), win_s[1], st(win_p[2]), win_s[2], st(pool_p), st(pool_s))
```

```python
import functools

import jax
import jax.numpy as jnp
from jax import lax
from jax.experimental import pallas as pl
from jax.experimental.pallas import tpu as pltpu

F32 = jnp.float32
BF16 = jnp.bfloat16
HI = lax.Precision.HIGHEST
EPS = 1e-6
NEG_INF = float("-inf")

D_MODEL = 1024
DN_HEADS = 4
DN_DIM = 128
DN_CHUNK = 64
CONV_W = 4
SWA_GROUPS = ((128, 1), (512, 4), (2048, 16))
SWA_HEADS = 4
SWA_DIM = 64
SWA_OUT = SWA_HEADS * SWA_DIM
SWA_BLOCK = 128
ROPE_THETA = 10000.0
POOL_WINDOWS = (2, 4, 8, 16)
POOL_GROUP = 384
POOL_WIDTH = 1536
POOL_BUF = 15
PEER_KEYS = 128
PEER_HEADS = 8
PEER_TOPK = 16
PEER_SEL = PEER_HEADS * PEER_TOPK
PAST_LEN = 2048

COL_QKV_A = 0
COL_U = 1536
COL_GATE = 3072
COL_Z = 6144
COL_QKV_B = 6656
COL_AB = 8960
P_COLS = 9216
VMEM_LIMIT = 56 * 1024 * 1024


def _cparams(sem):
    return pltpu.CompilerParams(dimension_semantics=sem, vmem_limit_bytes=VMEM_LIMIT)


def _sigmoid(x):
    return 1.0 / (1.0 + jnp.exp(-x))


def _silu(x):
    return x * _sigmoid(x)


def _inproj_kernel(x_ref, g_ref, w_ref, o_ref, h_scr):
    @pl.when(pl.program_id(1) == 0)
    def _():
        x = x_ref[...]
        y = x * lax.rsqrt(jnp.mean(x * x, axis=-1, keepdims=True) + EPS)
        h_scr[...] = (y * g_ref[...]).astype(BF16)

    o_ref[...] = jnp.dot(h_scr[...], w_ref[...], preferred_element_type=F32)


def _inproj(x, g, w_bf, tm=1024, tn=1024):
    n, d = x.shape
    nc = w_bf.shape[1]
    return pl.pallas_call(
        _inproj_kernel,
        grid=(n // tm, nc // tn),
        in_specs=[pl.BlockSpec((tm, d), lambda i, j: (i, 0)),
                  pl.BlockSpec((1, d), lambda i, j: (0, 0)),
                  pl.BlockSpec((d, tn), lambda i, j: (0, j))],
        out_specs=pl.BlockSpec((tm, tn), lambda i, j: (i, j)),
        out_shape=jax.ShapeDtypeStruct((n, nc), F32),
        scratch_shapes=[pltpu.VMEM((tm, d), BF16)],
        compiler_params=_cparams(("parallel", "arbitrary")),
        name="inproj",
    )(x, g.reshape(1, d), w_bf)


def _pack_w_in(w):
    d = w.shape[0]
    qkv_b = w[:, 2056:4360].reshape(d, 3, len(SWA_GROUPS), SWA_OUT).transpose(0, 2, 1, 3).reshape(d, 2304)
    parts = [w[:, 0:1536], w[:, 4360:5896], w[:, 5896:8968], w[:, 1536:2048], qkv_b, w[:, 2048:2056],
             jnp.zeros((d, P_COLS - 8968), w.dtype)]
    return jnp.concatenate(parts, axis=1).astype(BF16)


def _qkvb_col(gi, which):
    return COL_QKV_B + (gi * 3 + which) * SWA_OUT


def _dn_kernel(alog_ref, dtb_ref, q_ref, k_ref, v_ref, z_ref, ab_ref, abT_ref, c0_ref, cw_ref, s0_ref, dnn_ref,
                o_ref, sout_ref, S_scr, ext_scr, *, C, Tc, nseq):
    H, DH = DN_HEADS, DN_DIM
    W = H * DH
    t = pl.program_id(1)

    @pl.when(t == 0)
    def _():
        S_scr[...] = s0_ref[...]
        ext_scr[:, 0:8, :] = c0_ref[...]

    cw = cw_ref[...]
    acts = []
    for sq in range(nseq):
        rs = slice(sq * Tc, (sq + 1) * Tc)
        ext_scr[sq, 8:8 + Tc, 0:W] = q_ref[rs, :]
        ext_scr[sq, 8:8 + Tc, W:2 * W] = k_ref[rs, :]
        ext_scr[sq, 8:8 + Tc, 2 * W:3 * W] = v_ref[rs, :]
        acts.append(_silu(ext_scr[sq, 5:5 + Tc, :] * cw[0:1] + ext_scr[sq, 6:6 + Tc, :] * cw[1:2]
                          + ext_scr[sq, 7:7 + Tc, :] * cw[2:3] + ext_scr[sq, 8:8 + Tc, :] * cw[3:4]))
        ext_scr[sq, 0:8, :] = ext_scr[sq, Tc:Tc + 8, :]

    def l2n(x):
        return x * lax.rsqrt(jnp.sum(x * x, axis=-1, keepdims=True) + EPS)

    def softplus(x):
        return jnp.maximum(x, 0.0) + jnp.log1p(jnp.exp(-jnp.abs(x)))

    def split(a):
        hi = a.astype(BF16)
        return hi, (a - hi.astype(F32)).astype(BF16)

    def mm3(a, b, dims):
        (ah, al), (bh, bl) = split(a), split(b)

        def dg(x, y):
            return lax.dot_general(x, y, (dims, ((), ())), preferred_element_type=F32)

        return dg(ah, bh) + dg(ah, bl) + dg(al, bh)

    def mm(a, b):
        return mm3(a, b, ((1,), (0,)))

    def mm_nt(a, b):
        return mm3(a, b, ((1,), (1,)))

    def mm_tn(a, b):
        return mm3(a, b, ((0,), (0,)))

    ab_all = ab_ref[...]
    zs = _silu(z_ref[...])
    dnn = dnn_ref[...]
    ii = lax.broadcasted_iota(jnp.int32, (C, C), 0)
    jj = lax.broadcasted_iota(jnp.int32, (C, C), 1)
    incl = ii >= jj
    strict = ii > jj
    eye = jnp.where(ii == jj, 1.0, 0.0).astype(F32)
    nck = Tc // C

    pairs = []
    for sq in range(nseq):
        act = acts[sq]
        ab = ab_all[sq * Tc:(sq + 1) * Tc]
        abT = abT_ref[sq]
        for h in range(H):
            qh = l2n(act[:, h * DH:(h + 1) * DH]) * (DH ** -0.5)
            kh = l2n(act[:, W + h * DH:W + (h + 1) * DH])
            vh = act[:, 2 * W + h * DH:2 * W + (h + 1) * DH]
            neg_a = -jnp.exp(jnp.full((1, 1), alog_ref[h], F32))
            g_col_h = neg_a * softplus(ab[:, h:h + 1] + dtb_ref[h])
            g_row_h = neg_a * softplus(abT[h:h + 1, :] + dtb_ref[h])
            beta_h = _sigmoid(ab[:, H + h:H + h + 1])
            for c in range(nck):
                sl = slice(c * C, (c + 1) * C)
                qc, kc, vc = qh[sl], kh[sl], vh[sl]
                g_col, g_row, beta = g_col_h[sl], g_row_h[:, sl], beta_h[sl]
                gc_col = jnp.sum(jnp.where(incl, g_row, 0.0), axis=1, keepdims=True)
                gc_row = jnp.sum(jnp.where(ii <= jj, g_col, 0.0), axis=0, keepdims=True)
                gc_last = jnp.sum(g_row, axis=1, keepdims=True)
                decay = jnp.where(incl, jnp.exp(jnp.where(incl, gc_col - gc_row, 0.0)), 0.0)
                kb = kc * beta
                egc = jnp.exp(gc_col)
                pairs.append(dict(st=(sq, h), c=c, rows=slice(sq * Tc + c * C, sq * Tc + (c + 1) * C),
                                  qc=qc, kc=kc, kb=kb, vb=vc * beta, decay=decay, egc=egc,
                                  kdec=kc * jnp.exp(gc_last - gc_col), glast=jnp.exp(gc_last)))
    for p in pairs:
        p["pw"] = -(mm_nt(p["kb"], p["kc"]) * jnp.where(strict, p["decay"], 0.0))
        p["intra"] = mm_nt(p["qc"], p["kc"]) * p["decay"]
    for p in pairs:
        p["inv"] = eye + p["pw"]
    m = 2
    while m < C:
        for p in pairs:
            p["pw"] = mm(p["pw"], p["pw"])
        for p in pairs:
            p["inv"] = p["inv"] + mm(p["inv"], p["pw"])
        m *= 2
    for p in pairs:
        p["u"] = mm(p["inv"], p["vb"])
        p["w"] = mm(p["inv"], p["kb"] * p["egc"])
    S = {(sq, h): S_scr[sq, h] for sq in range(nseq) for h in range(H)}
    for c in range(nck):
        cur = [p for p in pairs if p["c"] == c]
        vnew = [p["u"] - mm(p["w"], S[p["st"]]) for p in cur]
        for p, vn in zip(cur, vnew):
            st = p["st"]
            cols = slice(st[1] * DH, (st[1] + 1) * DH)
            o = mm(p["qc"] * p["egc"], S[st]) + mm(p["intra"], vn)
            S[st] = S[st] * p["glast"] + mm_tn(p["kdec"], vn)
            o = o * lax.rsqrt(jnp.mean(o * o, axis=-1, keepdims=True) + EPS) * dnn
            o_ref[p["rows"], cols] = o * zs[p["rows"], cols]
    for (sq, h), val in S.items():
        S_scr[sq, h] = val

    @pl.when(t == pl.num_programs(1) - 1)
    def _():
        for (sq, h), val in S.items():
            sout_ref[sq, h] = val


def _deltanet(P, abT, conv0p, cw8, state0, a_log, dt_bias, dn_norm, row0, B, T, Tc, C, nseq=1):
    nt = T // Tc
    assert nseq == 1 or nt == 1
    rt = nseq * Tc
    rb0 = row0 // rt
    H, W = DN_HEADS, DN_HEADS * DN_DIM

    def rows(width, col):
        return pl.BlockSpec((rt, width), lambda b, t, c=col: (rb0 + b * nt + t, c))

    smem = pl.BlockSpec(memory_space=pltpu.SMEM)
    return pl.pallas_call(
        functools.partial(_dn_kernel, C=C, Tc=Tc, nseq=nseq),
        grid=(B // nseq, nt),
        in_specs=[smem, smem, rows(W, 0), rows(W, 1), rows(W, 2), rows(W, COL_Z // W), rows(128, COL_AB // 128),
                  pl.BlockSpec((nseq, 8, Tc), lambda b, t: (b, 0, t)),
                  pl.BlockSpec((nseq, 8, 3 * W), lambda b, t: (b, 0, 0)),
                  pl.BlockSpec((8, 3 * W), lambda b, t: (0, 0)),
                  pl.BlockSpec((nseq, H, DN_DIM, DN_DIM), lambda b, t: (b, 0, 0, 0)),
                  pl.BlockSpec((1, DN_DIM), lambda b, t: (0, 0))],
        out_specs=[pl.BlockSpec((rt, W), lambda b, t: (b * nt + t, 0)),
                   pl.BlockSpec((nseq, H, DN_DIM, DN_DIM), lambda b, t: (b, 0, 0, 0))],
        out_shape=[jax.ShapeDtypeStruct((B * T, W), F32), jax.ShapeDtypeStruct((B, H, DN_DIM, DN_DIM), F32)],
        scratch_shapes=[pltpu.VMEM((nseq, H, DN_DIM, DN_DIM), F32), pltpu.VMEM((nseq, Tc + 8, 3 * W), F32)],
        compiler_params=_cparams(("parallel", "arbitrary")),
        name="deltanet",
    )(a_log, dt_bias, P, P, P, P, P, abT, conv0p, cw8, state0, dn_norm.reshape(1, DN_DIM))


def _rope_tables(pos):
    half = SWA_DIM // 2
    inv = ROPE_THETA ** (-jnp.arange(half, dtype=F32) / half)
    ang = pos.astype(F32)[:, None] * inv[None, :]
    cos = jnp.tile(jnp.cos(ang), (1, 2 * SWA_HEADS))
    sin = jnp.tile(jnp.sin(ang), (1, 2 * SWA_HEADS))
    return cos, sin


def _rope(x, cos, sin):
    lane = lax.broadcasted_iota(jnp.int32, x.shape, 1)
    first = (lane % SWA_DIM) < (SWA_DIM // 2)
    w = x.shape[1]
    xr = jnp.where(first, -pltpu.roll(x, w - SWA_DIM // 2, 1), pltpu.roll(x, SWA_DIM // 2, 1))
    return x * cos + xr * sin


def _mm_nt(a, b):
    return lax.dot_general(a, b, (((1,), (1,)), ((), ())), precision=HI, preferred_element_type=F32)


def _attn_prompt_kernel(q_ref, kc_ref, kp_ref, vc_ref, vp_ref, cq_ref, sq_ref, cp_ref, sp_ref,
                        o_ref, m_ref, l_ref, kr_ref, *, n_back, nq):
    i = pl.program_id(2)
    Bq = SWA_BLOCK
    H = SWA_HEADS
    q_all = _rope(q_ref[...], cq_ref[...], sq_ref[...])
    k_cur = _rope(kc_ref[...], cq_ref[...], sq_ref[...])
    k_prev = _rope(kp_ref[...], cp_ref[...], sp_ref[...])
    kr_ref[...] = k_cur
    k_all = jnp.concatenate([k_prev, k_cur], axis=0).astype(BF16)
    v_all = jnp.concatenate([vp_ref[...], vc_ref[...]], axis=0).astype(BF16)
    qi = lax.broadcasted_iota(jnp.int32, (H * Bq, 2 * Bq), 0) % Bq
    kj = lax.broadcasted_iota(jnp.int32, (H * Bq, 2 * Bq), 1)
    rel = Bq + qi - kj
    in_band = (rel >= 0) & (rel <= n_back)
    head = lax.broadcasted_iota(jnp.int32, (Bq, SWA_OUT), 1) // SWA_DIM
    blocks = range(nq)
    scores = []
    for u in blocks:
        q = q_all[u * Bq:(u + 1) * Bq]
        qs = jnp.concatenate([jnp.where(head == h, q, 0.0) for h in range(H)], axis=0).astype(BF16)
        scores.append(lax.dot_general(qs, k_all[u * Bq:(u + 2) * Bq], (((1,), (1,)), ((), ())),
                                      preferred_element_type=F32))
    stats = []
    for u in blocks:
        valid = in_band & ((i * nq + u) * Bq + qi - rel >= 0)
        s = jnp.where(valid, scores[u] * (SWA_DIM ** -0.5), NEG_INF)
        m = jnp.max(s, axis=1, keepdims=True)
        p = jnp.exp(s - m)
        stats.append((m, jnp.sum(p, axis=1, keepdims=True), p.astype(BF16)))
    accs = [jnp.dot(stats[u][2], v_all[u * Bq:(u + 2) * Bq], preferred_element_type=F32) for u in blocks]
    for u in blocks:
        m, l, _ = stats[u]
        o = jnp.zeros((Bq, SWA_OUT), F32)
        m_full = jnp.zeros((Bq, SWA_OUT), F32)
        l_full = jnp.zeros((Bq, SWA_OUT), F32)
        for h in range(H):
            rows = slice(h * Bq, (h + 1) * Bq)
            o = jnp.where(head == h, accs[u][rows], o)
            m_full = jnp.where(head == h, m[rows], m_full)
            l_full = jnp.where(head == h, l[rows], l_full)
        out_rows = slice(u * Bq, (u + 1) * Bq)
        o_ref[out_rows, :] = o
        m_ref[out_rows, :] = m_full
        l_ref[out_rows, :] = l_full


def _attn_prompt(P, cos, sin, gi, B, T):
    window, dil = SWA_GROUPS[gi]
    n_back = window // dil
    assert n_back <= SWA_BLOCK
    L = T // dil
    nb = L // SWA_BLOCK
    if dil == 1:
        Pd, cw, c0 = P, 0, _qkvb_col(gi, 0) // SWA_OUT
    else:
        Pd = P[:B * T, _qkvb_col(gi, 0):_qkvb_col(gi, 0) + 3 * SWA_OUT].reshape(B * L, dil * 3 * SWA_OUT)
        cw, c0 = 3, 0
    cd = cos.reshape(L, dil * SWA_OUT)
    sd = sin.reshape(L, dil * SWA_OUT)
    nq = 2 if nb % 2 == 0 else 1
    ns = nb // nq
    blk = (nq * SWA_BLOCK, SWA_OUT)
    pblk = (SWA_BLOCK, SWA_OUT)

    def cur(which):
        return pl.BlockSpec(blk, lambda b, p, i, c=c0 + which: (b * ns + i, p * cw + c))

    def prev(which):
        return pl.BlockSpec(pblk, lambda b, p, i, c=c0 + which: (b * nb + jnp.maximum(i * nq - 1, 0), p * cw + c))

    tcur = pl.BlockSpec(blk, lambda b, p, i: (i, p))
    tprev = pl.BlockSpec(pblk, lambda b, p, i: (jnp.maximum(i * nq - 1, 0), p))
    ospec = pl.BlockSpec(blk, lambda b, p, i: (b * ns + i, p))
    oshape = jax.ShapeDtypeStruct((B * L, dil * SWA_OUT), F32)
    outs = pl.pallas_call(
        functools.partial(_attn_prompt_kernel, n_back=n_back, nq=nq),
        grid=(B, dil, ns),
        in_specs=[cur(0), cur(1), prev(1), cur(2), prev(2), tcur, tcur, tprev, tprev],
        out_specs=[ospec] * 4,
        out_shape=[oshape] * 4,
        compiler_params=_cparams(("parallel", "parallel", "arbitrary")),
        name=f"attn_prompt{gi}",
    )(Pd, Pd, Pd, Pd, Pd, cd, sd, cd, sd)
    return [a.reshape(B * T, SWA_OUT) for a in outs]


def _attn_sample_kernel(q_ref, k_ref, v_ref, cos_ref, sin_ref, kv_ref, o_ref, m_ref, l_ref, kr_ref, *, dil):
    T = q_ref.shape[0]
    W = kv_ref.shape[-1]
    H = SWA_HEADS
    q = _rope(q_ref[...], cos_ref[...], sin_ref[...])
    k_new = _rope(k_ref[...], cos_ref[...], sin_ref[...])
    v_new = v_ref[...]
    kr_ref[...] = k_new
    head = lax.broadcasted_iota(jnp.int32, (T, SWA_OUT), 1) // SWA_DIM
    tq = lax.broadcasted_iota(jnp.int32, (H * T, W), 0) % T
    wk = lax.broadcasted_iota(jnp.int32, (H * T, W), 1)
    valid_buf = (((W + tq - wk) & (dil - 1)) == 0) & (wk >= tq)
    tq2 = lax.broadcasted_iota(jnp.int32, (H * T, T), 0) % T
    tk2 = lax.broadcasted_iota(jnp.int32, (H * T, T), 1)
    valid_new = (((tq2 - tk2) & (dil - 1)) == 0) & (tk2 <= tq2)
    scale = SWA_DIM ** -0.5
    nt_dims = (((1,), (1,)), ((), ()))
    qs = jnp.concatenate([jnp.where(head == h, q, 0.0) for h in range(H)], axis=0).astype(BF16)
    k_t = kv_ref[0, 0, 0].reshape(H * SWA_DIM, W).astype(BF16)
    v_t = kv_ref[0, 0, 1].reshape(H * SWA_DIM, W).astype(BF16)
    sn = lax.dot_general(qs, k_new.astype(BF16), nt_dims, preferred_element_type=F32)
    sn = jnp.where(valid_new, sn * scale, NEG_INF)
    sb = jnp.where(valid_buf, jnp.dot(qs, k_t, preferred_element_type=F32) * scale, NEG_INF)
    m = jnp.maximum(jnp.max(sn, axis=1, keepdims=True), jnp.max(sb, axis=1, keepdims=True))
    pn = jnp.exp(sn - m)
    pb = jnp.exp(sb - m)
    l = jnp.sum(pn, axis=1, keepdims=True) + jnp.sum(pb, axis=1, keepdims=True)
    acc = (jnp.dot(pn.astype(BF16), v_new.astype(BF16), preferred_element_type=F32)
           + lax.dot_general(pb.astype(BF16), v_t, nt_dims, preferred_element_type=F32))
    o = jnp.zeros((T, SWA_OUT), F32)
    m_full = jnp.zeros((T, SWA_OUT), F32)
    l_full = jnp.zeros((T, SWA_OUT), F32)
    for h in range(H):
        rows = slice(h * T, (h + 1) * T)
        o = jnp.where(head == h, acc[rows], o)
        m_full = jnp.where(head == h, m[rows], m_full)
        l_full = jnp.where(head == h, l[rows], l_full)
    o_ref[...] = o
    m_ref[...] = m_full
    l_ref[...] = l_full


def _attn_sample(P, cos, sin, cache_t, layer, gi, row0, B, T):
    window, dil = SWA_GROUPS[gi]
    W = cache_t.shape[-1]
    assert W == window and W % dil == 0 and dil & (dil - 1) == 0 and row0 % T == 0
    rb0 = row0 // T
    blk = (T, SWA_OUT)

    def rows(which):
        return pl.BlockSpec(blk, lambda b, c=_qkvb_col(gi, which) // SWA_OUT: (rb0 + b, c))

    tab = pl.BlockSpec(blk, lambda b: (0, 0))
    ospec = pl.BlockSpec(blk, lambda b: (b, 0))
    oshape = jax.ShapeDtypeStruct((B * T, SWA_OUT), F32)
    return pl.pallas_call(
        functools.partial(_attn_sample_kernel, dil=dil),
        grid=(B,),
        in_specs=[rows(0), rows(1), rows(2), tab, tab,
                  pl.BlockSpec((1, 1, 2, SWA_HEADS, SWA_DIM, W), lambda b: (layer, b, 0, 0, 0, 0))],
        out_specs=[ospec] * 4,
        out_shape=[oshape] * 4,
        compiler_params=_cparams(("parallel",)),
        name=f"attn_sample{gi}",
    )(P, P, P, cos, sin, cache_t)


def _pool_kernel(u_ref, pre_ref, w_ref, sc_ref, o_ref, ext_scr, *, Tc, pos0):
    t = pl.program_id(1)
    nb = POOL_BUF + 1

    @pl.when(t == 0)
    def _():
        ext_scr[0:nb, :] = pre_ref[0]

    u = u_ref[...]
    ext_scr[nb:nb + Tc, :] = u
    pos = (pos0 + t * Tc + lax.broadcasted_iota(jnp.int32, (Tc, 1), 0)).astype(F32)
    for gi, w in enumerate(POOL_WINDOWS):
        lo, hi = gi * POOL_GROUP, (gi + 1) * POOL_GROUP
        ug = u[:, lo:hi]
        acc = ug
        for i in range(1, w):
            acc = acc + ext_scr[nb - i:nb - i + Tc, lo:hi]
        d = acc / jnp.minimum(float(w), pos + 1.0) - ug
        y = jnp.dot(d.astype(BF16), w_ref[gi], preferred_element_type=F32)
        o_ref[:, lo:hi] = y * sc_ref[:, lo:hi]
    ext_scr[0:nb, :] = ext_scr[Tc:Tc + nb, :]


def _pool(P, prefix16, w_pool_bf, pool_scale, row0, B, T, Tc, pos0):
    nt = T // Tc
    rb0 = row0 // Tc
    return pl.pallas_call(
        functools.partial(_pool_kernel, Tc=Tc, pos0=pos0),
        grid=(B, nt),
        in_specs=[pl.BlockSpec((Tc, POOL_WIDTH), lambda b, t: (rb0 + b * nt + t, COL_U // POOL_WIDTH)),
                  pl.BlockSpec((1, POOL_BUF + 1, POOL_WIDTH), lambda b, t: (b, 0, 0)),
                  pl.BlockSpec((len(POOL_WINDOWS), POOL_GROUP, POOL_GROUP), lambda b, t: (0, 0, 0)),
                  pl.BlockSpec((1, POOL_WIDTH), lambda b, t: (0, 0))],
        out_specs=pl.BlockSpec((Tc, POOL_WIDTH), lambda b, t: (b * nt + t, 0)),
        out_shape=jax.ShapeDtypeStruct((B * T, POOL_WIDTH), F32),
        scratch_shapes=[pltpu.VMEM((Tc + POOL_BUF + 1, POOL_WIDTH), F32)],
        compiler_params=_cparams(("parallel", "arbitrary")),
        name="pool",
    )(P, prefix16, w_pool_bf, pool_scale.reshape(1, POOL_WIDTH))


N_MIX = 11


def _merge_kernel(x_ref, g0_ref, g1_ref, g2_ref, *refs, np_blocks):
    mix_p, mix_s = refs[:N_MIX], refs[N_MIX:2 * N_MIX]
    wa_ref, wb_ref, wc_ref, wo_ref, nf_ref, x1_ref, h2b_ref, h2lo_ref = refs[2 * N_MIX:]
    is_p = pl.program_id(0) < np_blocks
    oa, o0, m0, l0, o1, m1, l1, o2, m2, l2, oc = [jnp.where(is_p, p[...], s[...]) for p, s in zip(mix_p, mix_s)]
    mx = jnp.maximum(jnp.maximum(m0, m1), m2)
    e0, e1, e2 = jnp.exp(m0 - mx), jnp.exp(m1 - mx), jnp.exp(m2 - mx)
    ob = (e0 * o0 + e1 * o1 + e2 * o2) / (e0 * l0 + e1 * l1 + e2 * l2)

    def proj(a, w_ref):
        return jnp.dot(a.astype(BF16), w_ref[...], preferred_element_type=F32)

    merged = (_sigmoid(g0_ref[...]) * proj(oa, wa_ref) + _sigmoid(g1_ref[...]) * proj(ob, wb_ref)
              + _sigmoid(g2_ref[...]) * proj(oc, wc_ref))
    x1 = x_ref[...] + proj(merged, wo_ref)
    x1_ref[...] = x1
    h2 = x1 * lax.rsqrt(jnp.mean(x1 * x1, axis=-1, keepdims=True) + EPS) * nf_ref[...]
    h2b = h2.astype(BF16)
    h2b_ref[...] = h2b
    h2lo_ref[...] = (h2 - h2b.astype(F32)).astype(BF16)


def _merge(x, P, mix_p, mix_s, wa, wb, wc, wo, norm_ffn, tm=256):
    n, d = x.shape
    npb = mix_p[0].shape[0] // tm
    nsb = mix_s[0].shape[0] // tm
    assert npb + nsb == n // tm

    def rows(width, col=0):
        return pl.BlockSpec((tm, width), lambda i, c=col: (i, c))

    def prow(a):
        return pl.BlockSpec((tm, a.shape[1]), lambda i: (jnp.minimum(i, npb - 1), 0))

    def srow(a):
        return pl.BlockSpec((tm, a.shape[1]), lambda i: (jnp.maximum(i - npb, 0), 0))

    def full(a):
        return pl.BlockSpec(a.shape, lambda i: (0,) * a.ndim)

    nf = norm_ffn.reshape(1, d)
    g0 = COL_GATE // d
    return pl.pallas_call(
        functools.partial(_merge_kernel, np_blocks=npb),
        grid=(n // tm,),
        in_specs=[rows(d), rows(d, g0), rows(d, g0 + 1), rows(d, g0 + 2)] + [prow(a) for a in mix_p]
        + [srow(a) for a in mix_s] + [full(wa), full(wb), full(wc), full(wo), full(nf)],
        out_specs=[rows(d), rows(d), rows(d)],
        out_shape=[jax.ShapeDtypeStruct((n, d), F32), jax.ShapeDtypeStruct((n, d), BF16),
                   jax.ShapeDtypeStruct((n, d), BF16)],
        compiler_params=_cparams(("parallel",)),
        name="merge",
    )(x, P, P, P, *mix_p, *mix_s, wa, wb, wc, wo, nf)


def _topk_rows(s, iota, k):
    vals, idxs = [], []
    big = 1e9
    for _ in range(k):
        m = jnp.max(s, axis=0, keepdims=True)
        idx = jnp.min(jnp.where(s == m, iota, big), axis=0, keepdims=True)
        vals.append(m)
        idxs.append(idx)
        s = jnp.where(iota == idx, NEG_INF, s)
    return jnp.concatenate(vals, axis=0), jnp.concatenate(idxs, axis=0)


def _peer_topk_kernel(hh_ref, hl_ref, wqh_ref, wql_ref, sk_ref, ia_ref, ib_ref, g_ref, q_scr, ia_scr, ib_scr, g_scr,
                      *, tn):
    TK = PEER_TOPK
    CH = 128
    q_scr[...] = (jnp.dot(hh_ref[...], wqh_ref[...], preferred_element_type=F32)
                  + jnp.dot(hh_ref[...], wql_ref[...], preferred_element_type=F32)
                  + jnp.dot(hl_ref[...], wqh_ref[...], preferred_element_type=F32))
    kio = lax.broadcasted_iota(jnp.int32, (PEER_KEYS, CH), 0).astype(F32)
    rho = lax.broadcasted_iota(jnp.int32, (TK + 8 * 8, CH), 0)
    mid = rho - TK
    cand_r = jnp.where(rho < TK, 0, jnp.where(mid < 56, (mid >> 3) + 1, mid - 56 + 8))
    cand_c = jnp.where(rho < TK, rho, jnp.where(mid < 56, mid & 7, 0))
    cand_ok = (cand_r + 1) * (cand_c + 1) <= TK
    cio = (cand_r * TK + cand_c).astype(F32)

    def chunk(j, carry):
        r0 = pl.multiple_of(j * CH, CH)

        def head(h, carry2):
            c0 = pl.multiple_of(h * 2 * PEER_KEYS, 2 * PEER_KEYS)
            q1 = q_scr[pl.ds(r0, CH), pl.ds(c0, PEER_KEYS)]
            q2 = q_scr[pl.ds(r0, CH), pl.ds(c0 + PEER_KEYS, PEER_KEYS)]
            s1, i1 = _topk_rows(_mm_nt(sk_ref[0], q1), kio, TK)
            s2, i2 = _topk_rows(_mm_nt(sk_ref[1], q2), kio, TK)
            cand = jnp.concatenate([s1[0:1] + s2] + [s1[r:r + 1] + s2[0:8] for r in range(1, 8)]
                                   + [s1[8:TK] + s2[0:1]], axis=0)
            best, ic = _topk_rows(jnp.where(cand_ok, cand, NEG_INF), cio, TK)
            rk = jnp.floor(ic * (1.0 / TK))
            ck = ic - rk * TK
            ia = jnp.zeros((TK, CH), F32)
            ib = jnp.zeros((TK, CH), F32)
            for r in range(TK):
                ia = jnp.where(rk == float(r), i1[r:r + 1], ia)
                ib = jnp.where(ck == float(r), i2[r:r + 1], ib)
            e = jnp.exp(best - best[0:1])
            gate = e / jnp.sum(e, axis=0, keepdims=True)
            o0 = pl.multiple_of(h * TK, TK)
            ia_scr[pl.ds(o0, TK), :] = ia
            ib_scr[pl.ds(o0, TK), :] = ib
            g_scr[pl.ds(o0, TK), :] = gate
            return carry2

        lax.fori_loop(0, PEER_HEADS, head, 0, unroll=4)
        ia_ref[pl.ds(r0, CH), :] = ia_scr[...].T.astype(jnp.int32)
        ib_ref[pl.ds(r0, CH), :] = ib_scr[...].T.astype(jnp.int32)
        g_ref[pl.ds(r0, CH), :] = g_scr[...].T
        return carry

    lax.fori_loop(0, tn // 128, chunk, 0)


def _peer_topk(h_hi, h_lo, w_query, subkeys, tn=256):
    n, d = h_hi.shape
    nq = w_query.shape[1]
    wq_hi = w_query.astype(BF16)
    wq_lo = (w_query - wq_hi.astype(F32)).astype(BF16)
    out = pl.BlockSpec((tn, PEER_SEL), lambda i: (i, 0))
    return pl.pallas_call(
        functools.partial(_peer_topk_kernel, tn=tn),
        grid=(n // tn,),
        in_specs=[pl.BlockSpec((tn, d), lambda i: (i, 0)),
                  pl.BlockSpec((tn, d), lambda i: (i, 0)),
                  pl.BlockSpec((d, nq), lambda i: (0, 0)),
                  pl.BlockSpec((d, nq), lambda i: (0, 0)),
                  pl.BlockSpec(subkeys.shape, lambda i: (0, 0, 0))],
        out_specs=[out, out, out],
        out_shape=[jax.ShapeDtypeStruct((n, PEER_SEL), jnp.int32), jax.ShapeDtypeStruct((n, PEER_SEL), jnp.int32),
                   jax.ShapeDtypeStruct((n, PEER_SEL), F32)],
        scratch_shapes=[pltpu.VMEM((tn, nq), F32), pltpu.VMEM((PEER_SEL, 128), F32),
                        pltpu.VMEM((PEER_SEL, 128), F32), pltpu.VMEM((PEER_SEL, 128), F32)],
        compiler_params=_cparams(("parallel",)),
        name="peer_topk",
    )(h_hi, h_lo, wq_hi, wq_lo, subkeys)


def _peer_up_kernel(h_ref, w_ref, o_ref):
    o_ref[...] = lax.dot_general(h_ref[...], w_ref[...], (((1,), (1,)), ((), ())),
                                 preferred_element_type=F32).astype(o_ref.dtype)


def _peer_up(h2b, w_up_bf, tm=1024, tn=1024):
    n, d = h2b.shape
    e = w_up_bf.shape[0]
    return pl.pallas_call(
        _peer_up_kernel,
        grid=(n // tm, e // tn),
        in_specs=[pl.BlockSpec((tm, d), lambda i, j: (i, 0)), pl.BlockSpec((tn, d), lambda i, j: (j, 0))],
        out_specs=pl.BlockSpec((tm, tn), lambda i, j: (i, j)),
        out_shape=jax.ShapeDtypeStruct((n, e), BF16),
        compiler_params=_cparams(("parallel", "arbitrary")),
        name="peer_up",
    )(h2b, w_up_bf)


def _peer_select_kernel(s_ref, ia_ref, ib_ref, g_ref, a_ref, x_scr, y_scr, *, tn, unroll):
    NK = PEER_KEYS
    SUB = 8
    U32 = jnp.uint32

    def unpack(a8, carry):
        col = pl.multiple_of(a8 * SUB * NK, SUB * NK)
        x = jnp.stack([pltpu.bitcast(s_ref[:, pl.ds(col + i * NK, NK)], U32) for i in range(SUB)], axis=0)
        x_scr[:, pl.ds(pl.multiple_of(a8 * SUB, SUB), SUB), :] = jnp.swapaxes(x, 0, 1)
        return carry

    lax.fori_loop(0, NK // SUB, unpack, 0)
    sub = lax.broadcasted_iota(jnp.int32, (NK, PEER_SEL), 0)

    def group(gidx, carry):
        pairs = [gidx * unroll + u for u in range(unroll)]
        toks = [2 * p + par for p in pairs for par in range(2)]
        is_a = [sub == ia_ref[pl.ds(t, 1), :] for t in toks]
        onehot_b = [jnp.where(sub == ib_ref[pl.ds(t, 1), :], 1.0, 0.0).astype(BF16) for t in toks]
        vals = []
        for u, p in enumerate(pairs):
            words = x_scr[p]
            for par in range(2):
                k = 2 * u + par
                x = pltpu.unpack_elementwise(words, index=par, packed_dtype=BF16, unpacked_dtype=F32)
                r = jnp.dot(x.astype(BF16), onehot_b[k], preferred_element_type=F32)
                vals.append(jnp.sum(jnp.where(is_a[k], r, 0.0), axis=0, keepdims=True))
        for u, p in enumerate(pairs):
            ys = []
            for par in range(2):
                k = 2 * u + par
                v = vals[k]
                act = g_ref[pl.ds(toks[k], 1), :] * (0.5 * v * (1.0 + lax.erf(v * (2.0 ** -0.5))))
                wa = jnp.where(is_a[k], act, 0.0).astype(BF16)
                ys.append(lax.dot_general(wa, onehot_b[k], (((1,), (1,)), ((), ())), preferred_element_type=F32))
            y_scr[p] = lax.bitcast_convert_type(pltpu.pack_elementwise(ys, packed_dtype=BF16), U32)
        return carry

    lax.fori_loop(0, tn // 2 // unroll, group, 0)

    def pack(a8, carry):
        col = pl.multiple_of(a8 * SUB * NK, SUB * NK)
        y = jnp.swapaxes(y_scr[:, pl.ds(pl.multiple_of(a8 * SUB, SUB), SUB), :], 0, 1)
        for i in range(SUB):
            a_ref[:, pl.ds(col + i * NK, NK)] = pltpu.bitcast(y[i], BF16)
        return carry

    lax.fori_loop(0, NK // SUB, pack, 0)


def _peer_select(s, ia, ib, gate, tn=128, unroll=8):
    n, e = s.shape
    big = pl.BlockSpec((tn, e), lambda i: (i, 0))
    small = pl.BlockSpec((tn, PEER_SEL), lambda i: (i, 0))
    return pl.pallas_call(
        functools.partial(_peer_select_kernel, tn=tn, unroll=unroll),
        grid=(n // tn,),
        in_specs=[big, small, small, small],
        out_specs=big,
        out_shape=jax.ShapeDtypeStruct(s.shape, BF16),
        scratch_shapes=[pltpu.VMEM((tn // 2, PEER_KEYS, PEER_KEYS), jnp.uint32)] * 2,
        compiler_params=_cparams(("parallel",)),
        name="peer_select",
    )(s, ia, ib, gate)


def _peer_down_kernel(a_ref, w_ref, x_ref, g_ref, o_ref, acc_scr, *, final_norm):
    k = pl.program_id(1)

    @pl.when(k == 0)
    def _():
        acc_scr[...] = jnp.zeros(acc_scr.shape, F32)

    acc_scr[...] += jnp.dot(a_ref[...], w_ref[...], preferred_element_type=F32)

    @pl.when(k == pl.num_programs(1) - 1)
    def _():
        x = x_ref[...] + acc_scr[...]
        if final_norm:
            x = x * lax.rsqrt(jnp.mean(x * x, axis=-1, keepdims=True) + EPS) * g_ref[...]
        o_ref[...] = x


def _peer_down(a, w_down_bf, x1, norm_g, final_norm, tm=1024, tk=1024):
    n, e = a.shape
    d = w_down_bf.shape[1]
    return pl.pallas_call(
        functools.partial(_peer_down_kernel, final_norm=final_norm),
        grid=(n // tm, e // tk),
        in_specs=[pl.BlockSpec((tm, tk), lambda i, k: (i, k)), pl.BlockSpec((tk, d), lambda i, k: (k, 0)),
                  pl.BlockSpec((tm, d), lambda i, k: (i, 0)), pl.BlockSpec((1, d), lambda i, k: (0, 0))],
        out_specs=pl.BlockSpec((tm, d), lambda i, k: (i, 0)),
        out_shape=jax.ShapeDtypeStruct((n, d), F32),
        scratch_shapes=[pltpu.VMEM((tm, d), F32)],
        compiler_params=_cparams(("parallel", "arbitrary")),
        name="peer_down",
    )(a, w_down_bf, x1, norm_g.reshape(1, d))


def _peer(x1, h2b, h2lo, w_query, subkeys, w_up, w_down, norm_g, final_norm):
    ia, ib, gate = _peer_topk(h2b, h2lo, w_query, subkeys)
    s = _peer_up(h2b, w_up.astype(BF16))
    a = _peer_select(s, ia, ib, gate)
    return _peer_down(a, w_down.astype(BF16), x1, norm_g, final_norm)


def kernel(x_prompt, x_sample, state_delta, state_conv, cache_win0, cache_win1, cache_win2, state_pool, norm_mix, w_in, conv_w, a_log, dt_bias, dn_norm, w_pool, pool_scale, w_branch, w_out, norm_ffn, peer_query, peer_subkeys, peer_up, peer_down, norm_final):
    Bp, Tp, D = x_prompt.shape
    Bs, Ts, _ = x_sample.shape
    depth = w_in.shape[0]
    past = PAST_LEN
    npr = Bp * Tp
    caches = (cache_win0, cache_win1, cache_win2)
    caches_t = [jnp.transpose(c, (0, 1, 3, 4, 5, 2)) for c in caches]
    x = jnp.concatenate([x_prompt.reshape(npr, D), x_sample.reshape(Bs * Ts, D)], axis=0)
    cos_p, sin_p = _rope_tables(jnp.arange(Tp))
    cos_s, sin_s = _rope_tables(past + jnp.arange(Ts))
    nconv = CONV_W - 1
    dn_w = DN_HEADS * DN_DIM

    delta_p, delta_s, conv_p, conv_s, pool_p, pool_s = [], [], [], [], [], []
    win_p = [[] for _ in SWA_GROUPS]
    new_s = [[] for _ in SWA_GROUPS]
    for l in range(depth):
        P = _inproj(x, norm_mix[l], _pack_w_in(w_in[l]))
        Ps = P[npr:].reshape(Bs, Ts, P_COLS)
        cw8 = jnp.pad(conv_w[l], ((0, 8 - CONV_W), (0, 0)))

        def prompt_tail(rows, col, width):
            return jnp.stack([lax.slice(P, ((b + 1) * Tp - rows, col), ((b + 1) * Tp, col + width)) for b in range(Bp)])

        abT_p = jnp.transpose(lax.slice(P, (0, COL_AB), (npr, COL_AB + 8)).reshape(Bp, Tp, 8), (0, 2, 1))
        abT_s = jnp.transpose(Ps[:, :, COL_AB:COL_AB + 8], (0, 2, 1))
        oa_p, dp = _deltanet(P, abT_p, jnp.zeros((Bp, 8, 3 * dn_w), F32), cw8,
                              jnp.zeros((Bp, DN_HEADS, DN_DIM, DN_DIM), F32), a_log[l], dt_bias[l], dn_norm[l],
                              0, Bp, Tp, 2 * DN_CHUNK, DN_CHUNK)
        oa_s, ds = _deltanet(P, abT_s, jnp.pad(state_conv[l], ((0, 0), (8 - nconv, 0), (0, 0))), cw8,
                              state_delta[l], a_log[l], dt_bias[l], dn_norm[l], npr, Bs, Ts, Ts, min(DN_CHUNK, Ts), nseq=4)
        delta_p.append(dp)
        delta_s.append(ds)
        conv_p.append(prompt_tail(nconv, COL_QKV_A, 3 * dn_w))
        conv_s.append(jnp.concatenate([state_conv[l], Ps[:, :, COL_QKV_A:COL_QKV_A + 3 * dn_w]], axis=1)[:, -nconv:])

        attn_p, attn_s = [], []
        for gi, (window, dil) in enumerate(SWA_GROUPS):
            o_p, m_p, l_p, kr_p = _attn_prompt(P, cos_p, sin_p, gi, Bp, Tp)
            o_s, m_s, l_s, kr_s = _attn_sample(P, cos_s, sin_s, caches_t[gi], l, gi, npr, Bs, Ts)
            attn_p += [o_p, m_p, l_p]
            attn_s += [o_s, m_s, l_s]
            vcol = _qkvb_col(gi, 2)
            keep = min(window, Tp)
            kv_p = jnp.stack([kr_p.reshape(Bp, Tp, SWA_OUT)[:, Tp - keep:], prompt_tail(keep, vcol, SWA_OUT)], axis=2)
            win_p[gi].append(kv_p.reshape(Bp, keep, 2, SWA_HEADS, SWA_DIM))
            kv_s = jnp.stack([kr_s.reshape(Bs, Ts, SWA_OUT), Ps[:, :, vcol:vcol + SWA_OUT]], axis=2)
            new_s[gi].append(kv_s.reshape(Bs, Ts, 2, SWA_HEADS, SWA_DIM))

        wp_bf = w_pool[l].astype(BF16)
        oc_p = _pool(P, jnp.zeros((Bp, POOL_BUF + 1, POOL_WIDTH), F32), wp_bf, pool_scale[l], 0, Bp, Tp, 256, 0)
        oc_s = _pool(P, jnp.pad(state_pool[l], ((0, 0), (1, 0), (0, 0))), wp_bf, pool_scale[l], npr, Bs, Ts, Ts, past)
        pool_p.append(prompt_tail(POOL_BUF, COL_U, POOL_WIDTH))
        pool_s.append(jnp.concatenate([state_pool[l], Ps[:, :, COL_U:COL_U + POOL_WIDTH]], axis=1)[:, -POOL_BUF:])

        wb = w_branch[l].astype(BF16)
        x1, h2b, h2lo = _merge(x, P, [oa_p] + attn_p + [oc_p], [oa_s] + attn_s + [oc_s],
                               wb[:dn_w], wb[dn_w:dn_w + SWA_OUT], wb[dn_w + SWA_OUT:], w_out[l].astype(BF16), norm_ffn[l])
        x = _peer(x1, h2b, h2lo, peer_query[l], peer_subkeys[l], peer_up[l], peer_down[l], norm_final, l == depth - 1)

    y_p = x[:npr].reshape(Bp, Tp, D)
    y_s = x[npr:].reshape(Bs, Ts, D)
    st = jnp.stack
    win_s = []
    for gi, (window, dil) in enumerate(SWA_GROUPS):
        kv_all = jnp.concatenate([caches[gi], st(new_s[gi])], axis=2)
        win_s.append(kv_all[:, :, kv_all.shape[2] - min(window, kv_all.shape[2]):])
    return (y_p, y_s, st(delta_p), st(delta_s), st(conv_p), st(conv_s),
            st(win_p[0]), win_s[0], st(win_p[1]), win_s[1], st(win_p[2]), win_s[2], st(pool_p), st(pool_s))
```

```python
import functools

import jax
import jax.numpy as jnp
from jax import lax
from jax.experimental import pallas as pl
from jax.experimental.pallas import tpu as pltpu

F32 = jnp.float32
BF16 = jnp.bfloat16
HI = lax.Precision.HIGHEST
EPS = 1e-6
NEG_INF = float("-inf")

D_MODEL = 1024
DN_HEADS = 4
DN_DIM = 128
DN_CHUNK = 64
CONV_W = 4
SWA_GROUPS = ((128, 1), (512, 4), (2048, 16))
SWA_HEADS = 4
SWA_DIM = 64
SWA_OUT = SWA_HEADS * SWA_DIM
SWA_BLOCK = 128
ROPE_THETA = 10000.0
POOL_WINDOWS = (2, 4, 8, 16)
POOL_GROUP = 384
POOL_WIDTH = 1536
POOL_BUF = 15
PEER_KEYS = 128
PEER_HEADS = 8
PEER_TOPK = 16
PEER_SEL = PEER_HEADS * PEER_TOPK
PAST_LEN = 2048

COL_QKV_A = 0
COL_U = 1536
COL_GATE = 3072
COL_Z = 6144
COL_QKV_B = 6656
COL_AB = 8960
P_COLS = 9216
VMEM_LIMIT = 56 * 1024 * 1024


def _cparams(sem):
    return pltpu.CompilerParams(dimension_semantics=sem, vmem_limit_bytes=VMEM_LIMIT)


def _sigmoid(x):
    return 1.0 / (1.0 + jnp.exp(-x))


def _silu(x):
    return x * _sigmoid(x)


def _inproj_kernel(x_ref, g_ref, w_ref, o_ref, h_scr):
    @pl.when(pl.program_id(1) == 0)
    def _():
        x = x_ref[...]
        y = x * lax.rsqrt(jnp.mean(x * x, axis=-1, keepdims=True) + EPS)
        h_scr[...] = (y * g_ref[...]).astype(BF16)

    o_ref[...] = jnp.dot(h_scr[...], w_ref[...], preferred_element_type=F32)


def _inproj(x, g, w_bf, tm=1024, tn=1024):
    n, d = x.shape
    nc = w_bf.shape[1]
    return pl.pallas_call(
        _inproj_kernel,
        grid=(n // tm, nc // tn),
        in_specs=[pl.BlockSpec((tm, d), lambda i, j: (i, 0)),
                  pl.BlockSpec((1, d), lambda i, j: (0, 0)),
                  pl.BlockSpec((d, tn), lambda i, j: (0, j))],
        out_specs=pl.BlockSpec((tm, tn), lambda i, j: (i, j)),
        out_shape=jax.ShapeDtypeStruct((n, nc), F32),
        scratch_shapes=[pltpu.VMEM((tm, d), BF16)],
        compiler_params=_cparams(("parallel", "arbitrary")),
        name="inproj",
    )(x, g.reshape(1, d), w_bf)


def _pack_w_in(w):
    d = w.shape[0]
    qkv_b = w[:, 2056:4360].reshape(d, 3, len(SWA_GROUPS), SWA_OUT).transpose(0, 2, 1, 3).reshape(d, 2304)
    parts = [w[:, 0:1536], w[:, 4360:5896], w[:, 5896:8968], w[:, 1536:2048], qkv_b, w[:, 2048:2056],
             jnp.zeros((d, P_COLS - 8968), w.dtype)]
    return jnp.concatenate(parts, axis=1).astype(BF16)


def _qkvb_col(gi, which):
    return COL_QKV_B + (gi * 3 + which) * SWA_OUT


def _dn_kernel(alog_ref, dtb_ref, q_ref, k_ref, v_ref, z_ref, ab_ref, abT_ref, c0_ref, cw_ref, s0_ref, dnn_ref,
                o_ref, sout_ref, S_scr, ext_scr, *, C, Tc, nseq):
    H, DH = DN_HEADS, DN_DIM
    W = H * DH
    t = pl.program_id(1)

    @pl.when(t == 0)
    def _():
        S_scr[...] = s0_ref[...]
        ext_scr[:, 0:8, :] = c0_ref[...]

    cw = cw_ref[...]
    acts = []
    for sq in range(nseq):
        rs = slice(sq * Tc, (sq + 1) * Tc)
        ext_scr[sq, 8:8 + Tc, 0:W] = q_ref[rs, :]
        ext_scr[sq, 8:8 + Tc, W:2 * W] = k_ref[rs, :]
        ext_scr[sq, 8:8 + Tc, 2 * W:3 * W] = v_ref[rs, :]
        acts.append(_silu(ext_scr[sq, 5:5 + Tc, :] * cw[0:1] + ext_scr[sq, 6:6 + Tc, :] * cw[1:2]
                          + ext_scr[sq, 7:7 + Tc, :] * cw[2:3] + ext_scr[sq, 8:8 + Tc, :] * cw[3:4]))
        ext_scr[sq, 0:8, :] = ext_scr[sq, Tc:Tc + 8, :]

    def l2n(x):
        return x * lax.rsqrt(jnp.sum(x * x, axis=-1, keepdims=True) + EPS)

    def softplus(x):
        return jnp.maximum(x, 0.0) + jnp.log1p(jnp.exp(-jnp.abs(x)))

    def split(a):
        hi = a.astype(BF16)
        return hi, (a - hi.astype(F32)).astype(BF16)

    def mm3(a, b, dims):
        (ah, al), (bh, bl) = split(a), split(b)

        def dg(x, y):
            return lax.dot_general(x, y, (dims, ((), ())), preferred_element_type=F32)

        return dg(ah, bh) + dg(ah, bl) + dg(al, bh)

    def mm(a, b):
        return mm3(a, b, ((1,), (0,)))

    def mm_nt(a, b):
        return mm3(a, b, ((1,), (1,)))

    def mm_tn(a, b):
        return mm3(a, b, ((0,), (0,)))

    ab_all = ab_ref[...]
    zs = _silu(z_ref[...])
    dnn = dnn_ref[...]
    ii = lax.broadcasted_iota(jnp.int32, (C, C), 0)
    jj = lax.broadcasted_iota(jnp.int32, (C, C), 1)
    incl = ii >= jj
    strict = ii > jj
    eye = jnp.where(ii == jj, 1.0, 0.0).astype(F32)
    nck = Tc // C

    pairs = []
    for sq in range(nseq):
        act = acts[sq]
        ab = ab_all[sq * Tc:(sq + 1) * Tc]
        abT = abT_ref[sq]
        for h in range(H):
            qh = l2n(act[:, h * DH:(h + 1) * DH]) * (DH ** -0.5)
            kh = l2n(act[:, W + h * DH:W + (h + 1) * DH])
            vh = act[:, 2 * W + h * DH:2 * W + (h + 1) * DH]
            neg_a = -jnp.exp(jnp.full((1, 1), alog_ref[h], F32))
            g_col_h = neg_a * softplus(ab[:, h:h + 1] + dtb_ref[h])
            g_row_h = neg_a * softplus(abT[h:h + 1, :] + dtb_ref[h])
            beta_h = _sigmoid(ab[:, H + h:H + h + 1])
            for c in range(nck):
                sl = slice(c * C, (c + 1) * C)
                qc, kc, vc = qh[sl], kh[sl], vh[sl]
                g_col, g_row, beta = g_col_h[sl], g_row_h[:, sl], beta_h[sl]
                gc_col = jnp.sum(jnp.where(incl, g_row, 0.0), axis=1, keepdims=True)
                gc_row = jnp.sum(jnp.where(ii <= jj, g_col, 0.0), axis=0, keepdims=True)
                gc_last = jnp.sum(g_row, axis=1, keepdims=True)
                decay = jnp.where(incl, jnp.exp(jnp.where(incl, gc_col - gc_row, 0.0)), 0.0)
                kb = kc * beta
                egc = jnp.exp(gc_col)
                pairs.append(dict(st=(sq, h), c=c, rows=slice(sq * Tc + c * C, sq * Tc + (c + 1) * C),
                                  qc=qc, kc=kc, kb=kb, vb=vc * beta, decay=decay, egc=egc,
                                  kdec=kc * jnp.exp(gc_last - gc_col), glast=jnp.exp(gc_last)))
    for p in pairs:
        p["pw"] = -(mm_nt(p["kb"], p["kc"]) * jnp.where(strict, p["decay"], 0.0))
        p["intra"] = mm_nt(p["qc"], p["kc"]) * p["decay"]
    for p in pairs:
        p["inv"] = eye + p["pw"]
    m = 2
    while m < C:
        for p in pairs:
            p["pw"] = mm(p["pw"], p["pw"])
        for p in pairs:
            p["inv"] = p["inv"] + mm(p["inv"], p["pw"])
        m *= 2
    for p in pairs:
        p["u"] = mm(p["inv"], p["vb"])
        p["w"] = mm(p["inv"], p["kb"] * p["egc"])
    S = {(sq, h): S_scr[sq, h] for sq in range(nseq) for h in range(H)}
    for c in range(nck):
        cur = [p for p in pairs if p["c"] == c]
        vnew = [p["u"] - mm(p["w"], S[p["st"]]) for p in cur]
        for p, vn in zip(cur, vnew):
            st = p["st"]
            cols = slice(st[1] * DH, (st[1] + 1) * DH)
            o = mm(p["qc"] * p["egc"], S[st]) + mm(p["intra"], vn)
            S[st] = S[st] * p["glast"] + mm_tn(p["kdec"], vn)
            o = o * lax.rsqrt(jnp.mean(o * o, axis=-1, keepdims=True) + EPS) * dnn
            o_ref[p["rows"], cols] = o * zs[p["rows"], cols]
    for (sq, h), val in S.items():
        S_scr[sq, h] = val

    @pl.when(t == pl.num_programs(1) - 1)
    def _():
        for (sq, h), val in S.items():
            sout_ref[sq, h] = val


def _deltanet(P, abT, conv0p, cw8, state0, a_log, dt_bias, dn_norm, row0, B, T, Tc, C, nseq=1):
    nt = T // Tc
    assert nseq == 1 or nt == 1
    rt = nseq * Tc
    rb0 = row0 // rt
    H, W = DN_HEADS, DN_HEADS * DN_DIM

    def rows(width, col):
        return pl.BlockSpec((rt, width), lambda b, t, c=col: (rb0 + b * nt + t, c))

    smem = pl.BlockSpec(memory_space=pltpu.SMEM)
    return pl.pallas_call(
        functools.partial(_dn_kernel, C=C, Tc=Tc, nseq=nseq),
        grid=(B // nseq, nt),
        in_specs=[smem, smem, rows(W, 0), rows(W, 1), rows(W, 2), rows(W, COL_Z // W), rows(128, COL_AB // 128),
                  pl.BlockSpec((nseq, 8, Tc), lambda b, t: (b, 0, t)),
                  pl.BlockSpec((nseq, 8, 3 * W), lambda b, t: (b, 0, 0)),
                  pl.BlockSpec((8, 3 * W), lambda b, t: (0, 0)),
                  pl.BlockSpec((nseq, H, DN_DIM, DN_DIM), lambda b, t: (b, 0, 0, 0)),
                  pl.BlockSpec((1, DN_DIM), lambda b, t: (0, 0))],
        out_specs=[pl.BlockSpec((rt, W), lambda b, t: (b * nt + t, 0)),
                   pl.BlockSpec((nseq, H, DN_DIM, DN_DIM), lambda b, t: (b, 0, 0, 0))],
        out_shape=[jax.ShapeDtypeStruct((B * T, W), F32), jax.ShapeDtypeStruct((B, H, DN_DIM, DN_DIM), F32)],
        scratch_shapes=[pltpu.VMEM((nseq, H, DN_DIM, DN_DIM), F32), pltpu.VMEM((nseq, Tc + 8, 3 * W), F32)],
        compiler_params=_cparams(("parallel", "arbitrary")),
        name="deltanet",
    )(a_log, dt_bias, P, P, P, P, P, abT, conv0p, cw8, state0, dn_norm.reshape(1, DN_DIM))


def _rope_tables(pos):
    half = SWA_DIM // 2
    inv = ROPE_THETA ** (-jnp.arange(half, dtype=F32) / half)
    ang = pos.astype(F32)[:, None] * inv[None, :]
    cos = jnp.tile(jnp.cos(ang), (1, 2 * SWA_HEADS))
    sin = jnp.tile(jnp.sin(ang), (1, 2 * SWA_HEADS))
    return cos, sin


def _rope(x, cos, sin):
    lane = lax.broadcasted_iota(jnp.int32, x.shape, 1)
    first = (lane % SWA_DIM) < (SWA_DIM // 2)
    w = x.shape[1]
    xr = jnp.where(first, -pltpu.roll(x, w - SWA_DIM // 2, 1), pltpu.roll(x, SWA_DIM // 2, 1))
    return x * cos + xr * sin


def _mm_nt(a, b):
    return lax.dot_general(a, b, (((1,), (1,)), ((), ())), precision=HI, preferred_element_type=F32)


def _attn_prompt_kernel(q_ref, kc_ref, kp_ref, vc_ref, vp_ref, cq_ref, sq_ref, cp_ref, sp_ref,
                        o_ref, m_ref, l_ref, kr_ref, *, n_back, nq):
    i = pl.program_id(2)
    Bq = SWA_BLOCK
    H = SWA_HEADS
    q_all = _rope(q_ref[...], cq_ref[...], sq_ref[...])
    k_cur = _rope(kc_ref[...], cq_ref[...], sq_ref[...])
    k_prev = _rope(kp_ref[...], cp_ref[...], sp_ref[...])
    kr_ref[...] = k_cur
    k_all = jnp.concatenate([k_prev, k_cur], axis=0).astype(BF16)
    v_all = jnp.concatenate([vp_ref[...], vc_ref[...]], axis=0).astype(BF16)
    qi = lax.broadcasted_iota(jnp.int32, (H * Bq, 2 * Bq), 0) % Bq
    kj = lax.broadcasted_iota(jnp.int32, (H * Bq, 2 * Bq), 1)
    rel = Bq + qi - kj
    in_band = (rel >= 0) & (rel <= n_back)
    head = lax.broadcasted_iota(jnp.int32, (Bq, SWA_OUT), 1) // SWA_DIM
    blocks = range(nq)
    scores = []
    for u in blocks:
        q = q_all[u * Bq:(u + 1) * Bq]
        qs = jnp.concatenate([jnp.where(head == h, q, 0.0) for h in range(H)], axis=0).astype(BF16)
        scores.append(lax.dot_general(qs, k_all[u * Bq:(u + 2) * Bq], (((1,), (1,)), ((), ())),
                                      preferred_element_type=F32))
    stats = []
    for u in blocks:
        valid = in_band & ((i * nq + u) * Bq + qi - rel >= 0)
        s = jnp.where(valid, scores[u] * (SWA_DIM ** -0.5), NEG_INF)
        m = jnp.max(s, axis=1, keepdims=True)
        p = jnp.exp(s - m)
        stats.append((m, jnp.sum(p, axis=1, keepdims=True), p.astype(BF16)))
    accs = [jnp.dot(stats[u][2], v_all[u * Bq:(u + 2) * Bq], preferred_element_type=F32) for u in blocks]
    for u in blocks:
        m, l, _ = stats[u]
        o = jnp.zeros((Bq, SWA_OUT), F32)
        m_full = jnp.zeros((Bq, SWA_OUT), F32)
        l_full = jnp.zeros((Bq, SWA_OUT), F32)
        for h in range(H):
            rows = slice(h * Bq, (h + 1) * Bq)
            o = jnp.where(head == h, accs[u][rows], o)
            m_full = jnp.where(head == h, m[rows], m_full)
            l_full = jnp.where(head == h, l[rows], l_full)
        out_rows = slice(u * Bq, (u + 1) * Bq)
        o_ref[out_rows, :] = o
        m_ref[out_rows, :] = m_full
        l_ref[out_rows, :] = l_full


def _attn_prompt(P, cos, sin, gi, B, T):
    window, dil = SWA_GROUPS[gi]
    n_back = window // dil
    assert n_back <= SWA_BLOCK
    L = T // dil
    nb = L // SWA_BLOCK
    if dil == 1:
        Pd, cw, c0 = P, 0, _qkvb_col(gi, 0) // SWA_OUT
    else:
        Pd = P[:B * T, _qkvb_col(gi, 0):_qkvb_col(gi, 0) + 3 * SWA_OUT].reshape(B * L, dil * 3 * SWA_OUT)
        cw, c0 = 3, 0
    cd = cos.reshape(L, dil * SWA_OUT)
    sd = sin.reshape(L, dil * SWA_OUT)
    nq = 2 if nb % 2 == 0 else 1
    ns = nb // nq
    blk = (nq * SWA_BLOCK, SWA_OUT)
    pblk = (SWA_BLOCK, SWA_OUT)

    def cur(which):
        return pl.BlockSpec(blk, lambda b, p, i, c=c0 + which: (b * ns + i, p * cw + c))

    def prev(which):
        return pl.BlockSpec(pblk, lambda b, p, i, c=c0 + which: (b * nb + jnp.maximum(i * nq - 1, 0), p * cw + c))

    tcur = pl.BlockSpec(blk, lambda b, p, i: (i, p))
    tprev = pl.BlockSpec(pblk, lambda b, p, i: (jnp.maximum(i * nq - 1, 0), p))
    ospec = pl.BlockSpec(blk, lambda b, p, i: (b * ns + i, p))
    oshape = jax.ShapeDtypeStruct((B * L, dil * SWA_OUT), F32)
    outs = pl.pallas_call(
        functools.partial(_attn_prompt_kernel, n_back=n_back, nq=nq),
        grid=(B, dil, ns),
        in_specs=[cur(0), cur(1), prev(1), cur(2), prev(2), tcur, tcur, tprev, tprev],
        out_specs=[ospec] * 4,
        out_shape=[oshape] * 4,
        compiler_params=_cparams(("parallel", "parallel", "arbitrary")),
        name=f"attn_prompt{gi}",
    )(Pd, Pd, Pd, Pd, Pd, cd, sd, cd, sd)
    return [a.reshape(B * T, SWA_OUT) for a in outs]


def _attn_sample_kernel(q_ref, k_ref, v_ref, cos_ref, sin_ref, kv_ref, o_ref, m_ref, l_ref, kr_ref, *, dil):
    T = q_ref.shape[0]
    W = kv_ref.shape[-1]
    H = SWA_HEADS
    q = _rope(q_ref[...], cos_ref[...], sin_ref[...])
    k_new = _rope(k_ref[...], cos_ref[...], sin_ref[...])
    v_new = v_ref[...]
    kr_ref[...] = k_new
    head = lax.broadcasted_iota(jnp.int32, (T, SWA_OUT), 1) // SWA_DIM
    tq = lax.broadcasted_iota(jnp.int32, (H * T, W), 0) % T
    wk = lax.broadcasted_iota(jnp.int32, (H * T, W), 1)
    valid_buf = (((W + tq - wk) & (dil - 1)) == 0) & (wk >= tq)
    tq2 = lax.broadcasted_iota(jnp.int32, (H * T, T), 0) % T
    tk2 = lax.broadcasted_iota(jnp.int32, (H * T, T), 1)
    valid_new = (((tq2 - tk2) & (dil - 1)) == 0) & (tk2 <= tq2)
    scale = SWA_DIM ** -0.5
    nt_dims = (((1,), (1,)), ((), ()))
    qs = jnp.concatenate([jnp.where(head == h, q, 0.0) for h in range(H)], axis=0).astype(BF16)
    k_t = kv_ref[0, 0, 0].reshape(H * SWA_DIM, W).astype(BF16)
    v_t = kv_ref[0, 0, 1].reshape(H * SWA_DIM, W).astype(BF16)
    sn = lax.dot_general(qs, k_new.astype(BF16), nt_dims, preferred_element_type=F32)
    sn = jnp.where(valid_new, sn * scale, NEG_INF)
    sb = jnp.where(valid_buf, jnp.dot(qs, k_t, preferred_element_type=F32) * scale, NEG_INF)
    m = jnp.maximum(jnp.max(sn, axis=1, keepdims=True), jnp.max(sb, axis=1, keepdims=True))
    pn = jnp.exp(sn - m)
    pb = jnp.exp(sb - m)
    l = jnp.sum(pn, axis=1, keepdims=True) + jnp.sum(pb, axis=1, keepdims=True)
    acc = (jnp.dot(pn.astype(BF16), v_new.astype(BF16), preferred_element_type=F32)
           + lax.dot_general(pb.astype(BF16), v_t, nt_dims, preferred_element_type=F32))
    o = jnp.zeros((T, SWA_OUT), F32)
    m_full = jnp.zeros((T, SWA_OUT), F32)
    l_full = jnp.zeros((T, SWA_OUT), F32)
    for h in range(H):
        rows = slice(h * T, (h + 1) * T)
        o = jnp.where(head == h, acc[rows], o)
        m_full = jnp.where(head == h, m[rows], m_full)
        l_full = jnp.where(head == h, l[rows], l_full)
    o_ref[...] = o
    m_ref[...] = m_full
    l_ref[...] = l_full


def _attn_sample(P, cos, sin, cache_t, layer, gi, row0, B, T):
    window, dil = SWA_GROUPS[gi]
    W = cache_t.shape[-1]
    assert W == window and W % dil == 0 and dil & (dil - 1) == 0 and row0 % T == 0
    rb0 = row0 // T
    blk = (T, SWA_OUT)

    def rows(which):
        return pl.BlockSpec(blk, lambda b, c=_qkvb_col(gi, which) // SWA_OUT: (rb0 + b, c))

    tab = pl.BlockSpec(blk, lambda b: (0, 0))
    ospec = pl.BlockSpec(blk, lambda b: (b, 0))
    oshape = jax.ShapeDtypeStruct((B * T, SWA_OUT), F32)
    return pl.pallas_call(
        functools.partial(_attn_sample_kernel, dil=dil),
        grid=(B,),
        in_specs=[rows(0), rows(1), rows(2), tab, tab,
                  pl.BlockSpec((1, 1, 2, SWA_HEADS, SWA_DIM, W), lambda b: (layer, b, 0, 0, 0, 0))],
        out_specs=[ospec] * 4,
        out_shape=[oshape] * 4,
        compiler_params=_cparams(("parallel",)),
        name=f"attn_sample{gi}",
    )(P, P, P, cos, sin, cache_t)


def _pool_kernel(u_ref, pre_ref, w_ref, sc_ref, o_ref, ext_scr, *, Tc, pos0):
    t = pl.program_id(1)
    nb = POOL_BUF + 1

    @pl.when(t == 0)
    def _():
        ext_scr[0:nb, :] = pre_ref[0]

    u = u_ref[...]
    ext_scr[nb:nb + Tc, :] = u
    pos = (pos0 + t * Tc + lax.broadcasted_iota(jnp.int32, (Tc, 1), 0)).astype(F32)
    for gi, w in enumerate(POOL_WINDOWS):
        lo, hi = gi * POOL_GROUP, (gi + 1) * POOL_GROUP
        ug = u[:, lo:hi]
        acc = ug
        for i in range(1, w):
            acc = acc + ext_scr[nb - i:nb - i + Tc, lo:hi]
        d = acc / jnp.minimum(float(w), pos + 1.0) - ug
        y = jnp.dot(d.astype(BF16), w_ref[gi], preferred_element_type=F32)
        o_ref[:, lo:hi] = y * sc_ref[:, lo:hi]
    ext_scr[0:nb, :] = ext_scr[Tc:Tc + nb, :]


def _pool(P, prefix16, w_pool_bf, pool_scale, row0, B, T, Tc, pos0):
    nt = T // Tc
    rb0 = row0 // Tc
    return pl.pallas_call(
        functools.partial(_pool_kernel, Tc=Tc, pos0=pos0),
        grid=(B, nt),
        in_specs=[pl.BlockSpec((Tc, POOL_WIDTH), lambda b, t: (rb0 + b * nt + t, COL_U // POOL_WIDTH)),
                  pl.BlockSpec((1, POOL_BUF + 1, POOL_WIDTH), lambda b, t: (b, 0, 0)),
                  pl.BlockSpec((len(POOL_WINDOWS), POOL_GROUP, POOL_GROUP), lambda b, t: (0, 0, 0)),
                  pl.BlockSpec((1, POOL_WIDTH), lambda b, t: (0, 0))],
        out_specs=pl.BlockSpec((Tc, POOL_WIDTH), lambda b, t: (b * nt + t, 0)),
        out_shape=jax.ShapeDtypeStruct((B * T, POOL_WIDTH), F32),
        scratch_shapes=[pltpu.VMEM((Tc + POOL_BUF + 1, POOL_WIDTH), F32)],
        compiler_params=_cparams(("parallel", "arbitrary")),
        name="pool",
    )(P, prefix16, w_pool_bf, pool_scale.reshape(1, POOL_WIDTH))


N_MIX = 11


def _merge_kernel(x_ref, g0_ref, g1_ref, g2_ref, *refs, np_blocks):
    mix_p, mix_s = refs[:N_MIX], refs[N_MIX:2 * N_MIX]
    wa_ref, wb_ref, wc_ref, wo_ref, nf_ref, x1_ref, h2b_ref, h2lo_ref = refs[2 * N_MIX:]
    is_p = pl.program_id(0) < np_blocks
    oa, o0, m0, l0, o1, m1, l1, o2, m2, l2, oc = [jnp.where(is_p, p[...], s[...]) for p, s in zip(mix_p, mix_s)]
    mx = jnp.maximum(jnp.maximum(m0, m1), m2)
    e0, e1, e2 = jnp.exp(m0 - mx), jnp.exp(m1 - mx), jnp.exp(m2 - mx)
    ob = (e0 * o0 + e1 * o1 + e2 * o2) / (e0 * l0 + e1 * l1 + e2 * l2)

    def proj(a, w_ref):
        return jnp.dot(a.astype(BF16), w_ref[...], preferred_element_type=F32)

    merged = (_sigmoid(g0_ref[...]) * proj(oa, wa_ref) + _sigmoid(g1_ref[...]) * proj(ob, wb_ref)
              + _sigmoid(g2_ref[...]) * proj(oc, wc_ref))
    x1 = x_ref[...] + proj(merged, wo_ref)
    x1_ref[...] = x1
    h2 = x1 * lax.rsqrt(jnp.mean(x1 * x1, axis=-1, keepdims=True) + EPS) * nf_ref[...]
    h2b = h2.astype(BF16)
    h2b_ref[...] = h2b
    h2lo_ref[...] = (h2 - h2b.astype(F32)).astype(BF16)


def _merge(x, P, mix_p, mix_s, wa, wb, wc, wo, norm_ffn, tm=256):
    n, d = x.shape
    npb = mix_p[0].shape[0] // tm
    nsb = mix_s[0].shape[0] // tm
    assert npb + nsb == n // tm

    def rows(width, col=0):
        return pl.BlockSpec((tm, width), lambda i, c=col: (i, c))

    def prow(a):
        return pl.BlockSpec((tm, a.shape[1]), lambda i: (jnp.minimum(i, npb - 1), 0))

    def srow(a):
        return pl.BlockSpec((tm, a.shape[1]), lambda i: (jnp.maximum(i - npb, 0), 0))

    def full(a):
        return pl.BlockSpec(a.shape, lambda i: (0,) * a.ndim)

    nf = norm_ffn.reshape(1, d)
    g0 = COL_GATE // d
    return pl.pallas_call(
        functools.partial(_merge_kernel, np_blocks=npb),
        grid=(n // tm,),
        in_specs=[rows(d), rows(d, g0), rows(d, g0 + 1), rows(d, g0 + 2)] + [prow(a) for a in mix_p]
        + [srow(a) for a in mix_s] + [full(wa), full(wb), full(wc), full(wo), full(nf)],
        out_specs=[rows(d), rows(d), rows(d)],
        out_shape=[jax.ShapeDtypeStruct((n, d), F32), jax.ShapeDtypeStruct((n, d), BF16),
                   jax.ShapeDtypeStruct((n, d), BF16)],
        compiler_params=_cparams(("parallel",)),
        name="merge",
    )(x, P, P, P, *mix_p, *mix_s, wa, wb, wc, wo, nf)


def _topk_rows(s, iota, k):
    vals, idxs = [], []
    big = 1e9
    for _ in range(k):
        m = jnp.max(s, axis=0, keepdims=True)
        idx = jnp.min(jnp.where(s == m, iota, big), axis=0, keepdims=True)
        vals.append(m)
        idxs.append(idx)
        s = jnp.where(iota == idx, NEG_INF, s)
    return jnp.concatenate(vals, axis=0), jnp.concatenate(idxs, axis=0)


TOPK_LANES = 128
TOPK_HEADS_PER_STEP = 4


def _query_3pass(hh_ref, hl_ref, wqh_ref, wql_ref):
    return (jnp.dot(hh_ref[...], wqh_ref[...], preferred_element_type=F32)
            + jnp.dot(hh_ref[...], wql_ref[...], preferred_element_type=F32)
            + jnp.dot(hl_ref[...], wqh_ref[...], preferred_element_type=F32))


def _topk_consts():
    TK, CH = PEER_TOPK, TOPK_LANES
    kio = lax.broadcasted_iota(jnp.int32, (PEER_KEYS, CH), 0).astype(F32)
    rho = lax.broadcasted_iota(jnp.int32, (TK + 8 * 8, CH), 0)
    mid = rho - TK
    cand_r = jnp.where(rho < TK, 0, jnp.where(mid < 56, (mid >> 3) + 1, mid - 56 + 8))
    cand_c = jnp.where(rho < TK, rho, jnp.where(mid < 56, mid & 7, 0))
    cand_ok = (cand_r + 1) * (cand_c + 1) <= TK
    cio = (cand_r * TK + cand_c).astype(F32)
    return kio, cand_ok, cio


def _subkey_scores(h, r0, q_scr, sk_ref):
    c0 = pl.multiple_of(h * 2 * PEER_KEYS, 2 * PEER_KEYS)
    q1 = q_scr[pl.ds(r0, TOPK_LANES), pl.ds(c0, PEER_KEYS)]
    q2 = q_scr[pl.ds(r0, TOPK_LANES), pl.ds(c0 + PEER_KEYS, PEER_KEYS)]
    return _mm_nt(sk_ref[0], q1), _mm_nt(sk_ref[1], q2)


def _topk_head(h, scores, consts, ia_scr, ib_scr, g_scr):
    TK, CH = PEER_TOPK, TOPK_LANES
    kio, cand_ok, cio = consts
    s1, i1 = _topk_rows(scores[0], kio, TK)
    s2, i2 = _topk_rows(scores[1], kio, TK)
    cand = jnp.concatenate([s1[0:1] + s2] + [s1[r:r + 1] + s2[0:8] for r in range(1, 8)]
                           + [s1[8:TK] + s2[0:1]], axis=0)
    best, ic = _topk_rows(jnp.where(cand_ok, cand, NEG_INF), cio, TK)
    rk = jnp.floor(ic * (1.0 / TK))
    ck = ic - rk * TK
    ia = jnp.zeros((TK, CH), F32)
    ib = jnp.zeros((TK, CH), F32)
    for r in range(TK):
        ia = jnp.where(rk == float(r), i1[r:r + 1], ia)
        ib = jnp.where(ck == float(r), i2[r:r + 1], ib)
    e = jnp.exp(best - best[0:1])
    gate = e / jnp.sum(e, axis=0, keepdims=True)
    o0 = pl.multiple_of(h * TK, TK)
    ia_scr[pl.ds(o0, TK), :] = ia
    ib_scr[pl.ds(o0, TK), :] = ib
    g_scr[pl.ds(o0, TK), :] = gate


def _topk_emit(r0, ia_scr, ib_scr, g_scr, ia_ref, ib_ref, g_ref):
    ia_ref[pl.ds(r0, TOPK_LANES), :] = ia_scr[...].T.astype(jnp.int32)
    ib_ref[pl.ds(r0, TOPK_LANES), :] = ib_scr[...].T.astype(jnp.int32)
    g_ref[pl.ds(r0, TOPK_LANES), :] = g_scr[...].T


PAIR_ROWS = 8
DOWN_CHUNKS = 4


def _to_pair_major(x):
    xb = x.astype(BF16)
    chunks = [pltpu.bitcast(xb[:, c * PEER_KEYS:(c + 1) * PEER_KEYS], jnp.uint32) for c in range(PAIR_ROWS)]
    return jnp.swapaxes(jnp.stack(chunks, axis=0), 0, 1)


def _from_pair_major(words):
    y = jnp.swapaxes(words, 0, 1)
    return jnp.concatenate([pltpu.bitcast(y[c], BF16) for c in range(PAIR_ROWS)], axis=1)


def _peer_score_kernel(hh_ref, hl_ref, wqh_ref, wql_ref, sk_ref, w_ref, s_ref, ia_ref, ib_ref, g_ref,
                       q_scr, ia_scr, ib_scr, g_scr):
    j = pl.program_id(1)
    groups = PEER_HEADS // TOPK_HEADS_PER_STEP

    @pl.when(j == 0)
    def _():
        q_scr[...] = _query_3pass(hh_ref, hl_ref, wqh_ref, wql_ref)

    r0 = pl.multiple_of((j // groups) * TOPK_LANES, TOPK_LANES)
    hg = j % groups
    heads = [hg * TOPK_HEADS_PER_STEP + u for u in range(TOPK_HEADS_PER_STEP)]
    scores = [_subkey_scores(h, r0, q_scr, sk_ref) for h in heads]
    consts = _topk_consts()
    wcols = w_ref.shape[0] // TOPK_HEADS_PER_STEP
    parts = []
    for u, (h, sc) in enumerate(zip(heads, scores)):
        parts.append(lax.dot_general(hh_ref[...], w_ref[u * wcols:(u + 1) * wcols, :], (((1,), (1,)), ((), ())),
                                     preferred_element_type=F32))
        _topk_head(h, sc, consts, ia_scr, ib_scr, g_scr)
    s_ref[...] = _to_pair_major(jnp.concatenate(parts, axis=1))

    @pl.when(hg == groups - 1)
    def _():
        _topk_emit(r0, ia_scr, ib_scr, g_scr, ia_ref, ib_ref, g_ref)


def _peer_score(h_hi, h_lo, w_query, subkeys, w_up_bf):
    n, d = h_hi.shape
    e = w_up_bf.shape[0]
    nq = w_query.shape[1]
    tn = PAIR_ROWS * PEER_KEYS
    steps = e // tn
    tm = (steps * TOPK_HEADS_PER_STEP // PEER_HEADS) * TOPK_LANES
    assert n % tm == 0 and steps * TOPK_HEADS_PER_STEP % PEER_HEADS == 0
    wq_hi = w_query.astype(BF16)
    wq_lo = (w_query - wq_hi.astype(F32)).astype(BF16)
    rows = pl.BlockSpec((tm, d), lambda i, j: (i, 0))
    wq_spec = pl.BlockSpec((d, nq), lambda i, j: (0, 0))
    sel = pl.BlockSpec((tm, PEER_SEL), lambda i, j: (i, 0))
    return pl.pallas_call(
        _peer_score_kernel,
        grid=(n // tm, steps),
        in_specs=[rows, rows, wq_spec, wq_spec, pl.BlockSpec(subkeys.shape, lambda i, j: (0, 0, 0)),
                  pl.BlockSpec((tn, d), lambda i, j: (j, 0))],
        out_specs=[pl.BlockSpec((tm // 2, PAIR_ROWS, PEER_KEYS), lambda i, j: (i, j, 0)), sel, sel, sel],
        out_shape=[jax.ShapeDtypeStruct((n // 2, e // PEER_KEYS, PEER_KEYS), jnp.uint32),
                   jax.ShapeDtypeStruct((n, PEER_SEL), jnp.int32), jax.ShapeDtypeStruct((n, PEER_SEL), jnp.int32),
                   jax.ShapeDtypeStruct((n, PEER_SEL), F32)],
        scratch_shapes=[pltpu.VMEM((tm, nq), F32), pltpu.VMEM((PEER_SEL, TOPK_LANES), F32),
                        pltpu.VMEM((PEER_SEL, TOPK_LANES), F32), pltpu.VMEM((PEER_SEL, TOPK_LANES), F32)],
        compiler_params=_cparams(("parallel", "arbitrary")),
        name="peer_score",
    )(h_hi, h_lo, wq_hi, wq_lo, subkeys, w_up_bf)


def _peer_select_kernel(s_ref, ia_ref, ib_ref, g_ref, a_ref, *, tn, unroll):
    NK = PEER_KEYS
    sub = lax.broadcasted_iota(jnp.int32, (NK, PEER_SEL), 0)

    def group(gidx, carry):
        pairs = [gidx * unroll + u for u in range(unroll)]
        toks = [2 * p + par for p in pairs for par in range(2)]
        is_a = [sub == ia_ref[pl.ds(t, 1), :] for t in toks]
        onehot_b = [jnp.where(sub == ib_ref[pl.ds(t, 1), :], 1.0, 0.0).astype(BF16) for t in toks]
        vals = []
        for u, p in enumerate(pairs):
            words = s_ref[p]
            for par in range(2):
                k = 2 * u + par
                x = pltpu.unpack_elementwise(words, index=par, packed_dtype=BF16, unpacked_dtype=F32)
                r = jnp.dot(x.astype(BF16), onehot_b[k], preferred_element_type=F32)
                vals.append(jnp.sum(jnp.where(is_a[k], r, 0.0), axis=0, keepdims=True))
        for u, p in enumerate(pairs):
            ys = []
            for par in range(2):
                k = 2 * u + par
                v = vals[k]
                act = g_ref[pl.ds(toks[k], 1), :] * (0.5 * v * (1.0 + lax.erf(v * (2.0 ** -0.5))))
                wa = jnp.where(is_a[k], act, 0.0).astype(BF16)
                ys.append(lax.dot_general(wa, onehot_b[k], (((1,), (1,)), ((), ())), preferred_element_type=F32))
            a_ref[p] = lax.bitcast_convert_type(pltpu.pack_elementwise(ys, packed_dtype=BF16), jnp.uint32)
        return carry

    lax.fori_loop(0, tn // 2 // unroll, group, 0)


def _peer_select(s, ia, ib, gate, tn=128, unroll=8):
    n = ia.shape[0]
    big = pl.BlockSpec((tn // 2, PEER_KEYS, PEER_KEYS), lambda i: (i, 0, 0))
    small = pl.BlockSpec((tn, PEER_SEL), lambda i: (i, 0))
    return pl.pallas_call(
        functools.partial(_peer_select_kernel, tn=tn, unroll=unroll),
        grid=(n // tn,),
        in_specs=[big, small, small, small],
        out_specs=big,
        out_shape=jax.ShapeDtypeStruct(s.shape, jnp.uint32),
        compiler_params=_cparams(("parallel",)),
        name="peer_select",
    )(s, ia, ib, gate)


def _peer_down_kernel(a_ref, w_ref, x_ref, g_ref, o_ref, acc_scr, *, final_norm):
    k = pl.program_id(1)

    @pl.when(k == 0)
    def _():
        acc_scr[...] = jnp.zeros(acc_scr.shape, F32)

    pairs = a_ref.shape[0] // DOWN_CHUNKS
    for r in range(DOWN_CHUNKS):
        lhs = _from_pair_major(a_ref[r * pairs:(r + 1) * pairs])
        acc_scr[2 * r * pairs:2 * (r + 1) * pairs, :] += jnp.dot(lhs, w_ref[...], preferred_element_type=F32)

    @pl.when(k == pl.num_programs(1) - 1)
    def _():
        x = x_ref[...] + acc_scr[...]
        if final_norm:
            x = x * lax.rsqrt(jnp.mean(x * x, axis=-1, keepdims=True) + EPS) * g_ref[...]
        o_ref[...] = x


def _peer_down(a, w_down_bf, x1, norm_g, final_norm, tm=1024):
    n = x1.shape[0]
    e, d = w_down_bf.shape
    tk = PAIR_ROWS * PEER_KEYS
    return pl.pallas_call(
        functools.partial(_peer_down_kernel, final_norm=final_norm),
        grid=(n // tm, e // tk),
        in_specs=[pl.BlockSpec((tm // 2, PAIR_ROWS, PEER_KEYS), lambda i, k: (i, k, 0)),
                  pl.BlockSpec((tk, d), lambda i, k: (k, 0)),
                  pl.BlockSpec((tm, d), lambda i, k: (i, 0)), pl.BlockSpec((1, d), lambda i, k: (0, 0))],
        out_specs=pl.BlockSpec((tm, d), lambda i, k: (i, 0)),
        out_shape=jax.ShapeDtypeStruct((n, d), F32),
        scratch_shapes=[pltpu.VMEM((tm, d), F32)],
        compiler_params=_cparams(("parallel", "arbitrary")),
        name="peer_down",
    )(a, w_down_bf, x1, norm_g.reshape(1, d))


def _peer(x1, h2b, h2lo, w_query, subkeys, w_up, w_down, norm_g, final_norm):
    s, ia, ib, gate = _peer_score(h2b, h2lo, w_query, subkeys, w_up.astype(BF16))
    a = _peer_select(s, ia, ib, gate)
    return _peer_down(a, w_down.astype(BF16), x1, norm_g, final_norm)


def kernel(x_prompt, x_sample, state_delta, state_conv, cache_win0, cache_win1, cache_win2, state_pool, norm_mix, w_in, conv_w, a_log, dt_bias, dn_norm, w_pool, pool_scale, w_branch, w_out, norm_ffn, peer_query, peer_subkeys, peer_up, peer_down, norm_final):
    Bp, Tp, D = x_prompt.shape
    Bs, Ts, _ = x_sample.shape
    depth = w_in.shape[0]
    past = PAST_LEN
    npr = Bp * Tp
    caches = (cache_win0, cache_win1, cache_win2)
    caches_t = [jnp.transpose(c, (0, 1, 3, 4, 5, 2)) for c in caches]
    x = jnp.concatenate([x_prompt.reshape(npr, D), x_sample.reshape(Bs * Ts, D)], axis=0)
    cos_p, sin_p = _rope_tables(jnp.arange(Tp))
    cos_s, sin_s = _rope_tables(past + jnp.arange(Ts))
    nconv = CONV_W - 1
    dn_w = DN_HEADS * DN_DIM

    delta_p, delta_s, conv_p, conv_s, pool_p, pool_s = [], [], [], [], [], []
    win_p = [[] for _ in SWA_GROUPS]
    new_s = [[] for _ in SWA_GROUPS]
    for l in range(depth):
        P = _inproj(x, norm_mix[l], _pack_w_in(w_in[l]))
        Ps = P[npr:].reshape(Bs, Ts, P_COLS)
        cw8 = jnp.pad(conv_w[l], ((0, 8 - CONV_W), (0, 0)))

        def prompt_tail(rows, col, width):
            return jnp.stack([lax.slice(P, ((b + 1) * Tp - rows, col), ((b + 1) * Tp, col + width)) for b in range(Bp)])

        abT_p = jnp.transpose(lax.slice(P, (0, COL_AB), (npr, COL_AB + 8)).reshape(Bp, Tp, 8), (0, 2, 1))
        abT_s = jnp.transpose(Ps[:, :, COL_AB:COL_AB + 8], (0, 2, 1))
        oa_p, dp = _deltanet(P, abT_p, jnp.zeros((Bp, 8, 3 * dn_w), F32), cw8,
                              jnp.zeros((Bp, DN_HEADS, DN_DIM, DN_DIM), F32), a_log[l], dt_bias[l], dn_norm[l],
                              0, Bp, Tp, 2 * DN_CHUNK, DN_CHUNK)
        oa_s, ds = _deltanet(P, abT_s, jnp.pad(state_conv[l], ((0, 0), (8 - nconv, 0), (0, 0))), cw8,
                              state_delta[l], a_log[l], dt_bias[l], dn_norm[l], npr, Bs, Ts, Ts, min(DN_CHUNK, Ts), nseq=4)
        delta_p.append(dp)
        delta_s.append(ds)
        conv_p.append(prompt_tail(nconv, COL_QKV_A, 3 * dn_w))
        conv_s.append(jnp.concatenate([state_conv[l], Ps[:, :, COL_QKV_A:COL_QKV_A + 3 * dn_w]], axis=1)[:, -nconv:])

        attn_p, attn_s = [], []
        for gi, (window, dil) in enumerate(SWA_GROUPS):
            o_p, m_p, l_p, kr_p = _attn_prompt(P, cos_p, sin_p, gi, Bp, Tp)
            o_s, m_s, l_s, kr_s = _attn_sample(P, cos_s, sin_s, caches_t[gi], l, gi, npr, Bs, Ts)
            attn_p += [o_p, m_p, l_p]
            attn_s += [o_s, m_s, l_s]
            vcol = _qkvb_col(gi, 2)
            keep = min(window, Tp)
            kv_p = jnp.stack([kr_p.reshape(Bp, Tp, SWA_OUT)[:, Tp - keep:], prompt_tail(keep, vcol, SWA_OUT)], axis=2)
            win_p[gi].append(kv_p.reshape(Bp, keep, 2, SWA_HEADS, SWA_DIM))
            kv_s = jnp.stack([kr_s.reshape(Bs, Ts, SWA_OUT), Ps[:, :, vcol:vcol + SWA_OUT]], axis=2)
            new_s[gi].append(kv_s.reshape(Bs, Ts, 2, SWA_HEADS, SWA_DIM))

        wp_bf = w_pool[l].astype(BF16)
        oc_p = _pool(P, jnp.zeros((Bp, POOL_BUF + 1, POOL_WIDTH), F32), wp_bf, pool_scale[l], 0, Bp, Tp, 256, 0)
        oc_s = _pool(P, jnp.pad(state_pool[l], ((0, 0), (1, 0), (0, 0))), wp_bf, pool_scale[l], npr, Bs, Ts, Ts, past)
        pool_p.append(prompt_tail(POOL_BUF, COL_U, POOL_WIDTH))
        pool_s.append(jnp.concatenate([state_pool[l], Ps[:, :, COL_U:COL_U + POOL_WIDTH]], axis=1)[:, -POOL_BUF:])

        wb = w_branch[l].astype(BF16)
        x1, h2b, h2lo = _merge(x, P, [oa_p] + attn_p + [oc_p], [oa_s] + attn_s + [oc_s],
                               wb[:dn_w], wb[dn_w:dn_w + SWA_OUT], wb[dn_w + SWA_OUT:], w_out[l].astype(BF16), norm_ffn[l])
        x = _peer(x1, h2b, h2lo, peer_query[l], peer_subkeys[l], peer_up[l], peer_down[l], norm_final, l == depth - 1)

    y_p = x[:npr].reshape(Bp, Tp, D)
    y_s = x[npr:].reshape(Bs, Ts, D)
    st = jnp.stack
    win_s = []
    for gi, (window, dil) in enumerate(SWA_GROUPS):
        kv_all = jnp.concatenate([caches[gi], st(new_s[gi])], axis=2)
        win_s.append(kv_all[:, :, kv_all.shape[2] - min(window, kv_all.shape[2]):])
    return (y_p, y_s, st(delta_p), st(delta_s), st(conv_p), st(conv_s),
            st(win_p[0]), win_s[0], st(win_p[1]), win_s[1], st(win_p[2]), win_s[2], st(pool_p), st(pool_s))
```

```python
import functools

import jax
import jax.numpy as jnp
from jax import lax
from jax.experimental import pallas as pl
from jax.experimental.pallas import tpu as pltpu

F32 = jnp.float32
BF16 = jnp.bfloat16
HI = lax.Precision.HIGHEST
EPS = 1e-6
NEG_INF = float("-inf")

D_MODEL = 1024
DN_HEADS = 4
DN_DIM = 128
DN_CHUNK = 64
CONV_W = 4
SWA_GROUPS = ((128, 1), (512, 4), (2048, 16))
SWA_HEADS = 4
SWA_DIM = 64
SWA_OUT = SWA_HEADS * SWA_DIM
SWA_BLOCK = 128
ROPE_THETA = 10000.0
POOL_WINDOWS = (2, 4, 8, 16)
POOL_GROUP = 384
POOL_WIDTH = 1536
POOL_BUF = 15
PEER_KEYS = 128
PEER_HEADS = 8
PEER_TOPK = 16
PEER_SEL = PEER_HEADS * PEER_TOPK
PAST_LEN = 2048

COL_QKV_A = 0
COL_U = 1536
COL_GATE = 3072
COL_Z = 6144
COL_QKV_B = 6656
COL_AB = 8960
P_COLS = 9216
VMEM_LIMIT = 56 * 1024 * 1024


def _cparams(sem):
    return pltpu.CompilerParams(dimension_semantics=sem, vmem_limit_bytes=VMEM_LIMIT)


def _sigmoid(x):
    return 1.0 / (1.0 + jnp.exp(-x))


def _silu(x):
    return x * _sigmoid(x)


def _inproj_kernel(x_ref, g_ref, w_ref, o_ref, h_scr):
    @pl.when(pl.program_id(1) == 0)
    def _():
        x = x_ref[...]
        y = x * lax.rsqrt(jnp.mean(x * x, axis=-1, keepdims=True) + EPS)
        h_scr[...] = (y * g_ref[...]).astype(BF16)

    o_ref[...] = jnp.dot(h_scr[...], w_ref[...], preferred_element_type=F32)


def _inproj(x, g, w_bf, tm=1024, tn=1024):
    n, d = x.shape
    nc = w_bf.shape[1]
    return pl.pallas_call(
        _inproj_kernel,
        grid=(n // tm, nc // tn),
        in_specs=[pl.BlockSpec((tm, d), lambda i, j: (i, 0)),
                  pl.BlockSpec((1, d), lambda i, j: (0, 0)),
                  pl.BlockSpec((d, tn), lambda i, j: (0, j))],
        out_specs=pl.BlockSpec((tm, tn), lambda i, j: (i, j)),
        out_shape=jax.ShapeDtypeStruct((n, nc), F32),
        scratch_shapes=[pltpu.VMEM((tm, d), BF16)],
        compiler_params=_cparams(("parallel", "arbitrary")),
        name="inproj",
    )(x, g.reshape(1, d), w_bf)


def _pack_w_in(w):
    d = w.shape[0]
    qkv_b = w[:, 2056:4360].reshape(d, 3, len(SWA_GROUPS), SWA_OUT).transpose(0, 2, 1, 3).reshape(d, 2304)
    parts = [w[:, 0:1536], w[:, 4360:5896], w[:, 5896:8968], w[:, 1536:2048], qkv_b, w[:, 2048:2056],
             jnp.zeros((d, P_COLS - 8968), w.dtype)]
    return jnp.concatenate(parts, axis=1).astype(BF16)


def _qkvb_col(gi, which):
    return COL_QKV_B + (gi * 3 + which) * SWA_OUT


def _dn_kernel(alog_ref, dtb_ref, q_ref, k_ref, v_ref, z_ref, ab_ref, abT_ref, c0_ref, cw_ref, s0_ref, dnn_ref,
                o_ref, sout_ref, S_scr, ext_scr, *, C, Tc, nseq):
    H, DH = DN_HEADS, DN_DIM
    W = H * DH
    t = pl.program_id(1)

    @pl.when(t == 0)
    def _():
        S_scr[...] = s0_ref[...]
        ext_scr[:, 0:8, :] = c0_ref[...]

    cw = cw_ref[...]
    acts = []
    for sq in range(nseq):
        rs = slice(sq * Tc, (sq + 1) * Tc)
        ext_scr[sq, 8:8 + Tc, 0:W] = q_ref[rs, :]
        ext_scr[sq, 8:8 + Tc, W:2 * W] = k_ref[rs, :]
        ext_scr[sq, 8:8 + Tc, 2 * W:3 * W] = v_ref[rs, :]
        acts.append(_silu(ext_scr[sq, 5:5 + Tc, :] * cw[0:1] + ext_scr[sq, 6:6 + Tc, :] * cw[1:2]
                          + ext_scr[sq, 7:7 + Tc, :] * cw[2:3] + ext_scr[sq, 8:8 + Tc, :] * cw[3:4]))
        ext_scr[sq, 0:8, :] = ext_scr[sq, Tc:Tc + 8, :]

    def l2n(x):
        return x * lax.rsqrt(jnp.sum(x * x, axis=-1, keepdims=True) + EPS)

    def softplus(x):
        return jnp.maximum(x, 0.0) + jnp.log1p(jnp.exp(-jnp.abs(x)))

    def split(a):
        hi = a.astype(BF16)
        return hi, (a - hi.astype(F32)).astype(BF16)

    def mm3(a, b, dims):
        (ah, al), (bh, bl) = split(a), split(b)

        def dg(x, y):
            return lax.dot_general(x, y, (dims, ((), ())), preferred_element_type=F32)

        return dg(ah, bh) + dg(ah, bl) + dg(al, bh)

    def mm(a, b):
        return mm3(a, b, ((1,), (0,)))

    def mm_nt(a, b):
        return mm3(a, b, ((1,), (1,)))

    def mm_tn(a, b):
        return mm3(a, b, ((0,), (0,)))

    ab_all = ab_ref[...]
    zs = _silu(z_ref[...])
    dnn = dnn_ref[...]
    ii = lax.broadcasted_iota(jnp.int32, (C, C), 0)
    jj = lax.broadcasted_iota(jnp.int32, (C, C), 1)
    incl = ii >= jj
    strict = ii > jj
    eye = jnp.where(ii == jj, 1.0, 0.0).astype(F32)
    nck = Tc // C

    pairs = []
    for sq in range(nseq):
        act = acts[sq]
        ab = ab_all[sq * Tc:(sq + 1) * Tc]
        abT = abT_ref[sq]
        for h in range(H):
            qh = l2n(act[:, h * DH:(h + 1) * DH]) * (DH ** -0.5)
            kh = l2n(act[:, W + h * DH:W + (h + 1) * DH])
            vh = act[:, 2 * W + h * DH:2 * W + (h + 1) * DH]
            neg_a = -jnp.exp(jnp.full((1, 1), alog_ref[h], F32))
            g_col_h = neg_a * softplus(ab[:, h:h + 1] + dtb_ref[h])
            g_row_h = neg_a * softplus(abT[h:h + 1, :] + dtb_ref[h])
            beta_h = _sigmoid(ab[:, H + h:H + h + 1])
            for c in range(nck):
                sl = slice(c * C, (c + 1) * C)
                qc, kc, vc = qh[sl], kh[sl], vh[sl]
                g_col, g_row, beta = g_col_h[sl], g_row_h[:, sl], beta_h[sl]
                gc_col = jnp.sum(jnp.where(incl, g_row, 0.0), axis=1, keepdims=True)
                gc_row = jnp.sum(jnp.where(ii <= jj, g_col, 0.0), axis=0, keepdims=True)
                gc_last = jnp.sum(g_row, axis=1, keepdims=True)
                decay = jnp.where(incl, jnp.exp(jnp.where(incl, gc_col - gc_row, 0.0)), 0.0)
                kb = kc * beta
                egc = jnp.exp(gc_col)
                pairs.append(dict(st=(sq, h), c=c, rows=slice(sq * Tc + c * C, sq * Tc + (c + 1) * C),
                                  qc=qc, kc=kc, kb=kb, vb=vc * beta, decay=decay, egc=egc,
                                  kdec=kc * jnp.exp(gc_last - gc_col), glast=jnp.exp(gc_last)))
    for p in pairs:
        p["pw"] = -(mm_nt(p["kb"], p["kc"]) * jnp.where(strict, p["decay"], 0.0))
        p["intra"] = mm_nt(p["qc"], p["kc"]) * p["decay"]
    for p in pairs:
        p["inv"] = eye + p["pw"]
    m = 2
    while m < C:
        for p in pairs:
            p["pw"] = mm(p["pw"], p["pw"])
        for p in pairs:
            p["inv"] = p["inv"] + mm(p["inv"], p["pw"])
        m *= 2
    for p in pairs:
        p["u"] = mm(p["inv"], p["vb"])
        p["w"] = mm(p["inv"], p["kb"] * p["egc"])
    S = {(sq, h): S_scr[sq, h] for sq in range(nseq) for h in range(H)}
    for c in range(nck):
        cur = [p for p in pairs if p["c"] == c]
        vnew = [p["u"] - mm(p["w"], S[p["st"]]) for p in cur]
        for p, vn in zip(cur, vnew):
            st = p["st"]
            cols = slice(st[1] * DH, (st[1] + 1) * DH)
            o = mm(p["qc"] * p["egc"], S[st]) + mm(p["intra"], vn)
            S[st] = S[st] * p["glast"] + mm_tn(p["kdec"], vn)
            o = o * lax.rsqrt(jnp.mean(o * o, axis=-1, keepdims=True) + EPS) * dnn
            o_ref[p["rows"], cols] = o * zs[p["rows"], cols]
    for (sq, h), val in S.items():
        S_scr[sq, h] = val

    @pl.when(t == pl.num_programs(1) - 1)
    def _():
        for (sq, h), val in S.items():
            sout_ref[sq, h] = val


def _deltanet(P, abT, conv0p, cw8, state0, a_log, dt_bias, dn_norm, row0, B, T, Tc, C, nseq=1):
    nt = T // Tc
    assert nseq == 1 or nt == 1
    rt = nseq * Tc
    rb0 = row0 // rt
    H, W = DN_HEADS, DN_HEADS * DN_DIM

    def rows(width, col):
        return pl.BlockSpec((rt, width), lambda b, t, c=col: (rb0 + b * nt + t, c))

    smem = pl.BlockSpec(memory_space=pltpu.SMEM)
    return pl.pallas_call(
        functools.partial(_dn_kernel, C=C, Tc=Tc, nseq=nseq),
        grid=(B // nseq, nt),
        in_specs=[smem, smem, rows(W, 0), rows(W, 1), rows(W, 2), rows(W, COL_Z // W), rows(128, COL_AB // 128),
                  pl.BlockSpec((nseq, 8, Tc), lambda b, t: (b, 0, t)),
                  pl.BlockSpec((nseq, 8, 3 * W), lambda b, t: (b, 0, 0)),
                  pl.BlockSpec((8, 3 * W), lambda b, t: (0, 0)),
                  pl.BlockSpec((nseq, H, DN_DIM, DN_DIM), lambda b, t: (b, 0, 0, 0)),
                  pl.BlockSpec((1, DN_DIM), lambda b, t: (0, 0))],
        out_specs=[pl.BlockSpec((rt, W), lambda b, t: (b * nt + t, 0)),
                   pl.BlockSpec((nseq, H, DN_DIM, DN_DIM), lambda b, t: (b, 0, 0, 0))],
        out_shape=[jax.ShapeDtypeStruct((B * T, W), F32), jax.ShapeDtypeStruct((B, H, DN_DIM, DN_DIM), F32)],
        scratch_shapes=[pltpu.VMEM((nseq, H, DN_DIM, DN_DIM), F32), pltpu.VMEM((nseq, Tc + 8, 3 * W), F32)],
        compiler_params=_cparams(("parallel", "arbitrary")),
        name="deltanet",
    )(a_log, dt_bias, P, P, P, P, P, abT, conv0p, cw8, state0, dn_norm.reshape(1, DN_DIM))


def _rope_tables(pos):
    half = SWA_DIM // 2
    inv = ROPE_THETA ** (-jnp.arange(half, dtype=F32) / half)
    ang = pos.astype(F32)[:, None] * inv[None, :]
    cos = jnp.tile(jnp.cos(ang), (1, 2 * SWA_HEADS))
    sin = jnp.tile(jnp.sin(ang), (1, 2 * SWA_HEADS))
    return cos, sin


def _rope(x, cos, sin):
    lane = lax.broadcasted_iota(jnp.int32, x.shape, 1)
    first = (lane % SWA_DIM) < (SWA_DIM // 2)
    w = x.shape[1]
    xr = jnp.where(first, -pltpu.roll(x, w - SWA_DIM // 2, 1), pltpu.roll(x, SWA_DIM // 2, 1))
    return x * cos + xr * sin


def _mm_nt(a, b):
    return lax.dot_general(a, b, (((1,), (1,)), ((), ())), precision=HI, preferred_element_type=F32)


def _low_half(shape):
    return (lax.broadcasted_iota(jnp.int32, shape, 1) % SWA_DIM) < (SWA_DIM // 2)


def _pack_ml(m_full, l_full):
    return jnp.where(_low_half(m_full.shape), m_full, l_full)


def _unpack_ml(ml):
    low = _low_half(ml.shape)
    w = ml.shape[1]
    m = jnp.where(low, ml, pltpu.roll(ml, SWA_DIM // 2, 1))
    l = jnp.where(low, pltpu.roll(ml, w - SWA_DIM // 2, 1), ml)
    return m, l


def _attn_prompt_kernel(q_ref, kc_ref, kp_ref, vc_ref, vp_ref, cq_ref, sq_ref, cp_ref, sp_ref,
                        o_ref, ml_ref, kr_ref, *, n_back, nq):
    i = pl.program_id(2)
    Bq = SWA_BLOCK
    H = SWA_HEADS
    q_all = _rope(q_ref[...], cq_ref[...], sq_ref[...])
    k_cur = _rope(kc_ref[...], cq_ref[...], sq_ref[...])
    k_prev = _rope(kp_ref[...], cp_ref[...], sp_ref[...])
    kr_ref[...] = k_cur
    k_all = jnp.concatenate([k_prev, k_cur], axis=0).astype(BF16)
    v_all = jnp.concatenate([vp_ref[...], vc_ref[...]], axis=0).astype(BF16)
    qi = lax.broadcasted_iota(jnp.int32, (H * Bq, 2 * Bq), 0) % Bq
    kj = lax.broadcasted_iota(jnp.int32, (H * Bq, 2 * Bq), 1)
    rel = Bq + qi - kj
    in_band = (rel >= 0) & (rel <= n_back)
    head = lax.broadcasted_iota(jnp.int32, (Bq, SWA_OUT), 1) // SWA_DIM
    blocks = range(nq)
    scores = []
    for u in blocks:
        q = q_all[u * Bq:(u + 1) * Bq]
        qs = jnp.concatenate([jnp.where(head == h, q, 0.0) for h in range(H)], axis=0).astype(BF16)
        scores.append(lax.dot_general(qs, k_all[u * Bq:(u + 2) * Bq], (((1,), (1,)), ((), ())),
                                      preferred_element_type=F32))
    stats = []
    for u in blocks:
        valid = in_band & ((i * nq + u) * Bq + qi - rel >= 0)
        s = jnp.where(valid, scores[u] * (SWA_DIM ** -0.5), NEG_INF)
        m = jnp.max(s, axis=1, keepdims=True)
        p = jnp.exp(s - m)
        stats.append((m, jnp.sum(p, axis=1, keepdims=True), p.astype(BF16)))
    accs = [jnp.dot(stats[u][2], v_all[u * Bq:(u + 2) * Bq], preferred_element_type=F32) for u in blocks]
    for u in blocks:
        m, l, _ = stats[u]
        o = jnp.zeros((Bq, SWA_OUT), F32)
        m_full = jnp.zeros((Bq, SWA_OUT), F32)
        l_full = jnp.zeros((Bq, SWA_OUT), F32)
        for h in range(H):
            rows = slice(h * Bq, (h + 1) * Bq)
            o = jnp.where(head == h, accs[u][rows], o)
            m_full = jnp.where(head == h, m[rows], m_full)
            l_full = jnp.where(head == h, l[rows], l_full)
        out_rows = slice(u * Bq, (u + 1) * Bq)
        o_ref[out_rows, :] = o
        ml_ref[out_rows, :] = _pack_ml(m_full, l_full)


def _attn_prompt(P, cos, sin, gi, B, T):
    window, dil = SWA_GROUPS[gi]
    n_back = window // dil
    assert n_back <= SWA_BLOCK
    L = T // dil
    nb = L // SWA_BLOCK
    if dil == 1:
        Pd, cw, c0 = P, 0, _qkvb_col(gi, 0) // SWA_OUT
    else:
        Pd = P[:B * T, _qkvb_col(gi, 0):_qkvb_col(gi, 0) + 3 * SWA_OUT].reshape(B * L, dil * 3 * SWA_OUT)
        cw, c0 = 3, 0
    cd = cos.reshape(L, dil * SWA_OUT)
    sd = sin.reshape(L, dil * SWA_OUT)
    nq = 4 if nb % 4 == 0 else 2 if nb % 2 == 0 else 1
    ns = nb // nq
    blk = (nq * SWA_BLOCK, SWA_OUT)
    pblk = (SWA_BLOCK, SWA_OUT)

    def cur(which):
        return pl.BlockSpec(blk, lambda b, p, i, c=c0 + which: (b * ns + i, p * cw + c))

    def prev(which):
        return pl.BlockSpec(pblk, lambda b, p, i, c=c0 + which: (b * nb + jnp.maximum(i * nq - 1, 0), p * cw + c))

    tcur = pl.BlockSpec(blk, lambda b, p, i: (i, p))
    tprev = pl.BlockSpec(pblk, lambda b, p, i: (jnp.maximum(i * nq - 1, 0), p))
    ospec = pl.BlockSpec(blk, lambda b, p, i: (b * ns + i, p))
    oshape = jax.ShapeDtypeStruct((B * L, dil * SWA_OUT), F32)
    outs = pl.pallas_call(
        functools.partial(_attn_prompt_kernel, n_back=n_back, nq=nq),
        grid=(B, dil, ns),
        in_specs=[cur(0), cur(1), prev(1), cur(2), prev(2), tcur, tcur, tprev, tprev],
        out_specs=[ospec] * 3,
        out_shape=[oshape] * 3,
        compiler_params=_cparams(("parallel", "parallel", "arbitrary")),
        name=f"attn_prompt{gi}",
    )(Pd, Pd, Pd, Pd, Pd, cd, sd, cd, sd)
    return [a.reshape(B * T, SWA_OUT) for a in outs]


def _attn_sample_kernel(q_ref, k_ref, v_ref, cos_ref, sin_ref, kv_ref, o_ref, ml_ref, kr_ref, *, dil, T):
    nseq = q_ref.shape[0] // T
    W = kv_ref.shape[-1]
    H = SWA_HEADS
    cos = jnp.concatenate([cos_ref[...]] * nseq, axis=0)
    sin = jnp.concatenate([sin_ref[...]] * nseq, axis=0)
    q_all = _rope(q_ref[...], cos, sin)
    k_all = _rope(k_ref[...], cos, sin)
    v_all = v_ref[...]
    kr_ref[...] = k_all
    head = lax.broadcasted_iota(jnp.int32, (T, SWA_OUT), 1) // SWA_DIM
    tq = lax.broadcasted_iota(jnp.int32, (H * T, W), 0) % T
    wk = lax.broadcasted_iota(jnp.int32, (H * T, W), 1)
    valid_buf = (((W + tq - wk) & (dil - 1)) == 0) & (wk >= tq)
    tq2 = lax.broadcasted_iota(jnp.int32, (H * T, T), 0) % T
    tk2 = lax.broadcasted_iota(jnp.int32, (H * T, T), 1)
    valid_new = (((tq2 - tk2) & (dil - 1)) == 0) & (tk2 <= tq2)
    scale = SWA_DIM ** -0.5
    nt_dims = (((1,), (1,)), ((), ()))
    seqs = range(nseq)
    raw = []
    for sq in seqs:
        rs = slice(sq * T, (sq + 1) * T)
        qs = jnp.concatenate([jnp.where(head == h, q_all[rs], 0.0) for h in range(H)], axis=0).astype(BF16)
        k_t = kv_ref[0, sq, 0].reshape(H * SWA_DIM, W).astype(BF16)
        raw.append((lax.dot_general(qs, k_all[rs].astype(BF16), nt_dims, preferred_element_type=F32),
                    jnp.dot(qs, k_t, preferred_element_type=F32)))
    stats = []
    for sq in seqs:
        sn = jnp.where(valid_new, raw[sq][0] * scale, NEG_INF)
        sb = jnp.where(valid_buf, raw[sq][1] * scale, NEG_INF)
        m = jnp.maximum(jnp.max(sn, axis=1, keepdims=True), jnp.max(sb, axis=1, keepdims=True))
        pn = jnp.exp(sn - m)
        pb = jnp.exp(sb - m)
        stats.append((m, jnp.sum(pn, axis=1, keepdims=True) + jnp.sum(pb, axis=1, keepdims=True),
                      pn.astype(BF16), pb.astype(BF16)))
    accs = []
    for sq in seqs:
        rs = slice(sq * T, (sq + 1) * T)
        v_t = kv_ref[0, sq, 1].reshape(H * SWA_DIM, W).astype(BF16)
        accs.append(jnp.dot(stats[sq][2], v_all[rs].astype(BF16), preferred_element_type=F32)
                    + lax.dot_general(stats[sq][3], v_t, nt_dims, preferred_element_type=F32))
    for sq in seqs:
        m, l = stats[sq][0], stats[sq][1]
        o = jnp.zeros((T, SWA_OUT), F32)
        m_full = jnp.zeros((T, SWA_OUT), F32)
        l_full = jnp.zeros((T, SWA_OUT), F32)
        for h in range(H):
            rows = slice(h * T, (h + 1) * T)
            o = jnp.where(head == h, accs[sq][rows], o)
            m_full = jnp.where(head == h, m[rows], m_full)
            l_full = jnp.where(head == h, l[rows], l_full)
        rs = slice(sq * T, (sq + 1) * T)
        o_ref[rs, :] = o
        ml_ref[rs, :] = _pack_ml(m_full, l_full)


def _attn_sample(P, cos, sin, cache_t, layer, gi, row0, B, T):
    window, dil = SWA_GROUPS[gi]
    W = cache_t.shape[-1]
    nseq = 2
    assert W == window and W % dil == 0 and dil & (dil - 1) == 0 and row0 % (nseq * T) == 0 and B % nseq == 0
    rb0 = row0 // (nseq * T)
    blk = (nseq * T, SWA_OUT)

    def rows(which):
        return pl.BlockSpec(blk, lambda b, c=_qkvb_col(gi, which) // SWA_OUT: (rb0 + b, c))

    tab = pl.BlockSpec((T, SWA_OUT), lambda b: (0, 0))
    ospec = pl.BlockSpec(blk, lambda b: (b, 0))
    oshape = jax.ShapeDtypeStruct((B * T, SWA_OUT), F32)
    return pl.pallas_call(
        functools.partial(_attn_sample_kernel, dil=dil, T=T),
        grid=(B // nseq,),
        in_specs=[rows(0), rows(1), rows(2), tab, tab,
                  pl.BlockSpec((1, nseq, 2, SWA_HEADS, SWA_DIM, W), lambda b: (layer, b, 0, 0, 0, 0))],
        out_specs=[ospec] * 3,
        out_shape=[oshape] * 3,
        compiler_params=_cparams(("parallel",)),
        name=f"attn_sample{gi}",
    )(P, P, P, cos, sin, cache_t)


def _pool_kernel(u_ref, pre_ref, w_ref, sc_ref, o_ref, ext_scr, *, Tc, pos0):
    t = pl.program_id(1)
    nb = POOL_BUF + 1

    @pl.when(t == 0)
    def _():
        ext_scr[0:nb, :] = pre_ref[0]

    u = u_ref[...]
    ext_scr[nb:nb + Tc, :] = u
    pos = (pos0 + t * Tc + lax.broadcasted_iota(jnp.int32, (Tc, 1), 0)).astype(F32)
    for gi, w in enumerate(POOL_WINDOWS):
        lo, hi = gi * POOL_GROUP, (gi + 1) * POOL_GROUP
        ug = u[:, lo:hi]
        acc = ug
        for i in range(1, w):
            acc = acc + ext_scr[nb - i:nb - i + Tc, lo:hi]
        d = acc / jnp.minimum(float(w), pos + 1.0) - ug
        y = jnp.dot(d.astype(BF16), w_ref[gi], preferred_element_type=F32)
        o_ref[:, lo:hi] = y * sc_ref[:, lo:hi]
    ext_scr[0:nb, :] = ext_scr[Tc:Tc + nb, :]


def _pool(P, prefix16, w_pool_bf, pool_scale, row0, B, T, Tc, pos0):
    nt = T // Tc
    rb0 = row0 // Tc
    return pl.pallas_call(
        functools.partial(_pool_kernel, Tc=Tc, pos0=pos0),
        grid=(B, nt),
        in_specs=[pl.BlockSpec((Tc, POOL_WIDTH), lambda b, t: (rb0 + b * nt + t, COL_U // POOL_WIDTH)),
                  pl.BlockSpec((1, POOL_BUF + 1, POOL_WIDTH), lambda b, t: (b, 0, 0)),
                  pl.BlockSpec((len(POOL_WINDOWS), POOL_GROUP, POOL_GROUP), lambda b, t: (0, 0, 0)),
                  pl.BlockSpec((1, POOL_WIDTH), lambda b, t: (0, 0))],
        out_specs=pl.BlockSpec((Tc, POOL_WIDTH), lambda b, t: (b * nt + t, 0)),
        out_shape=jax.ShapeDtypeStruct((B * T, POOL_WIDTH), F32),
        scratch_shapes=[pltpu.VMEM((Tc + POOL_BUF + 1, POOL_WIDTH), F32)],
        compiler_params=_cparams(("parallel", "arbitrary")),
        name="pool",
    )(P, prefix16, w_pool_bf, pool_scale.reshape(1, POOL_WIDTH))


N_MIX = 8


def _merge_kernel(x_ref, g0_ref, g1_ref, g2_ref, *refs, np_blocks):
    mix_p, mix_s = refs[:N_MIX], refs[N_MIX:2 * N_MIX]
    wa_ref, wb_ref, wc_ref, wo_ref, nf_ref, x1_ref, h2b_ref, h2lo_ref = refs[2 * N_MIX:]
    is_p = pl.program_id(0) < np_blocks
    oa, o0, ml0, o1, ml1, o2, ml2, oc = [jnp.where(is_p, p[...], s[...]) for p, s in zip(mix_p, mix_s)]
    (m0, l0), (m1, l1), (m2, l2) = _unpack_ml(ml0), _unpack_ml(ml1), _unpack_ml(ml2)
    mx = jnp.maximum(jnp.maximum(m0, m1), m2)
    e0, e1, e2 = jnp.exp(m0 - mx), jnp.exp(m1 - mx), jnp.exp(m2 - mx)
    ob = (e0 * o0 + e1 * o1 + e2 * o2) / (e0 * l0 + e1 * l1 + e2 * l2)

    def proj(a, w_ref):
        return jnp.dot(a.astype(BF16), w_ref[...], preferred_element_type=F32)

    merged = (_sigmoid(g0_ref[...]) * proj(oa, wa_ref) + _sigmoid(g1_ref[...]) * proj(ob, wb_ref)
              + _sigmoid(g2_ref[...]) * proj(oc, wc_ref))
    x1 = x_ref[...] + proj(merged, wo_ref)
    x1_ref[...] = x1
    h2 = x1 * lax.rsqrt(jnp.mean(x1 * x1, axis=-1, keepdims=True) + EPS) * nf_ref[...]
    h2b = h2.astype(BF16)
    h2b_ref[...] = h2b
    h2lo_ref[...] = (h2 - h2b.astype(F32)).astype(BF16)


def _merge(x, P, mix_p, mix_s, wa, wb, wc, wo, norm_ffn, tm=256):
    n, d = x.shape
    npb = mix_p[0].shape[0] // tm
    nsb = mix_s[0].shape[0] // tm
    assert npb + nsb == n // tm

    def rows(width, col=0):
        return pl.BlockSpec((tm, width), lambda i, c=col: (i, c))

    def prow(a):
        return pl.BlockSpec((tm, a.shape[1]), lambda i: (jnp.minimum(i, npb - 1), 0))

    def srow(a):
        return pl.BlockSpec((tm, a.shape[1]), lambda i: (jnp.maximum(i - npb, 0), 0))

    def full(a):
        return pl.BlockSpec(a.shape, lambda i: (0,) * a.ndim)

    nf = norm_ffn.reshape(1, d)
    g0 = COL_GATE // d
    return pl.pallas_call(
        functools.partial(_merge_kernel, np_blocks=npb),
        grid=(n // tm,),
        in_specs=[rows(d), rows(d, g0), rows(d, g0 + 1), rows(d, g0 + 2)] + [prow(a) for a in mix_p]
        + [srow(a) for a in mix_s] + [full(wa), full(wb), full(wc), full(wo), full(nf)],
        out_specs=[rows(d), rows(d), rows(d)],
        out_shape=[jax.ShapeDtypeStruct((n, d), F32), jax.ShapeDtypeStruct((n, d), BF16),
                   jax.ShapeDtypeStruct((n, d), BF16)],
        compiler_params=_cparams(("parallel",)),
        name="merge",
    )(x, P, P, P, *mix_p, *mix_s, wa, wb, wc, wo, nf)


def _topk_rows(s, iota, k):
    vals, idxs = [], []
    big = 1e9
    for _ in range(k):
        m = jnp.max(s, axis=0, keepdims=True)
        idx = jnp.min(jnp.where(s == m, iota, big), axis=0, keepdims=True)
        vals.append(m)
        idxs.append(idx)
        s = jnp.where(iota == idx, NEG_INF, s)
    return jnp.concatenate(vals, axis=0), jnp.concatenate(idxs, axis=0)


TOPK_LANES = 128
TOPK_HEADS_PER_STEP = 4


def _query_3pass(hh_ref, hl_ref, wqh_ref, wql_ref):
    return (jnp.dot(hh_ref[...], wqh_ref[...], preferred_element_type=F32)
            + jnp.dot(hh_ref[...], wql_ref[...], preferred_element_type=F32)
            + jnp.dot(hl_ref[...], wqh_ref[...], preferred_element_type=F32))


def _topk_consts():
    TK, CH = PEER_TOPK, TOPK_LANES
    kio = lax.broadcasted_iota(jnp.int32, (PEER_KEYS, CH), 0).astype(F32)
    rho = lax.broadcasted_iota(jnp.int32, (TK + 8 * 8, CH), 0)
    mid = rho - TK
    cand_r = jnp.where(rho < TK, 0, jnp.where(mid < 56, (mid >> 3) + 1, mid - 56 + 8))
    cand_c = jnp.where(rho < TK, rho, jnp.where(mid < 56, mid & 7, 0))
    cand_ok = (cand_r + 1) * (cand_c + 1) <= TK
    cio = (cand_r * TK + cand_c).astype(F32)
    return kio, cand_ok, cio


def _subkey_scores(h, r0, q_scr, sk_ref):
    c0 = pl.multiple_of(h * 2 * PEER_KEYS, 2 * PEER_KEYS)
    q1 = q_scr[pl.ds(r0, TOPK_LANES), pl.ds(c0, PEER_KEYS)]
    q2 = q_scr[pl.ds(r0, TOPK_LANES), pl.ds(c0 + PEER_KEYS, PEER_KEYS)]
    return _mm_nt(sk_ref[0], q1), _mm_nt(sk_ref[1], q2)


def _topk_head(h, scores, consts, ia_scr, ib_scr, g_scr):
    TK, CH = PEER_TOPK, TOPK_LANES
    kio, cand_ok, cio = consts
    s1, i1 = _topk_rows(scores[0], kio, TK)
    s2, i2 = _topk_rows(scores[1], kio, TK)
    cand = jnp.concatenate([s1[0:1] + s2] + [s1[r:r + 1] + s2[0:8] for r in range(1, 8)]
                           + [s1[8:TK] + s2[0:1]], axis=0)
    best, ic = _topk_rows(jnp.where(cand_ok, cand, NEG_INF), cio, TK)
    rk = jnp.floor(ic * (1.0 / TK))
    ck = ic - rk * TK
    ia = jnp.zeros((TK, CH), F32)
    ib = jnp.zeros((TK, CH), F32)
    for r in range(TK):
        ia = jnp.where(rk == float(r), i1[r:r + 1], ia)
        ib = jnp.where(ck == float(r), i2[r:r + 1], ib)
    e = jnp.exp(best - best[0:1])
    gate = e / jnp.sum(e, axis=0, keepdims=True)
    o0 = pl.multiple_of(h * TK, TK)
    ia_scr[pl.ds(o0, TK), :] = ia
    ib_scr[pl.ds(o0, TK), :] = ib
    g_scr[pl.ds(o0, TK), :] = gate


def _topk_emit(r0, ia_scr, ib_scr, g_scr, ia_ref, ib_ref, g_ref):
    ia_ref[pl.ds(r0, TOPK_LANES), :] = ia_scr[...].T.astype(jnp.int32)
    ib_ref[pl.ds(r0, TOPK_LANES), :] = ib_scr[...].T.astype(jnp.int32)
    g_ref[pl.ds(r0, TOPK_LANES), :] = g_scr[...].T


PAIR_ROWS = 8
DOWN_CHUNKS = 4


def _to_pair_major(x):
    xb = x.astype(BF16)
    chunks = [pltpu.bitcast(xb[:, c * PEER_KEYS:(c + 1) * PEER_KEYS], jnp.uint32) for c in range(PAIR_ROWS)]
    return jnp.swapaxes(jnp.stack(chunks, axis=0), 0, 1)


def _from_pair_major(words):
    y = jnp.swapaxes(words, 0, 1)
    return jnp.concatenate([pltpu.bitcast(y[c], BF16) for c in range(PAIR_ROWS)], axis=1)


def _peer_score_kernel(hh_ref, hl_ref, wqh_ref, wql_ref, sk_ref, w_ref, s_ref, ia_ref, ib_ref, g_ref,
                       q_scr, ia_scr, ib_scr, g_scr):
    j = pl.program_id(1)
    groups = PEER_HEADS // TOPK_HEADS_PER_STEP

    @pl.when(j == 0)
    def _():
        q_scr[...] = _query_3pass(hh_ref, hl_ref, wqh_ref, wql_ref)

    r0 = pl.multiple_of((j // groups) * TOPK_LANES, TOPK_LANES)
    hg = j % groups
    heads = [hg * TOPK_HEADS_PER_STEP + u for u in range(TOPK_HEADS_PER_STEP)]
    scores = [_subkey_scores(h, r0, q_scr, sk_ref) for h in heads]
    consts = _topk_consts()
    wcols = w_ref.shape[0] // TOPK_HEADS_PER_STEP
    parts = []
    for u, (h, sc) in enumerate(zip(heads, scores)):
        parts.append(lax.dot_general(hh_ref[...], w_ref[u * wcols:(u + 1) * wcols, :], (((1,), (1,)), ((), ())),
                                     preferred_element_type=F32))
        _topk_head(h, sc, consts, ia_scr, ib_scr, g_scr)
    s_ref[...] = _to_pair_major(jnp.concatenate(parts, axis=1))

    @pl.when(hg == groups - 1)
    def _():
        _topk_emit(r0, ia_scr, ib_scr, g_scr, ia_ref, ib_ref, g_ref)


def _peer_score(h_hi, h_lo, w_query, subkeys, w_up_bf):
    n, d = h_hi.shape
    e = w_up_bf.shape[0]
    nq = w_query.shape[1]
    tn = PAIR_ROWS * PEER_KEYS
    steps = e // tn
    tm = (steps * TOPK_HEADS_PER_STEP // PEER_HEADS) * TOPK_LANES
    assert n % tm == 0 and steps * TOPK_HEADS_PER_STEP % PEER_HEADS == 0
    wq_hi = w_query.astype(BF16)
    wq_lo = (w_query - wq_hi.astype(F32)).astype(BF16)
    rows = pl.BlockSpec((tm, d), lambda i, j: (i, 0))
    wq_spec = pl.BlockSpec((d, nq), lambda i, j: (0, 0))
    sel = pl.BlockSpec((tm, PEER_SEL), lambda i, j: (i, 0))
    return pl.pallas_call(
        _peer_score_kernel,
        grid=(n // tm, steps),
        in_specs=[rows, rows, wq_spec, wq_spec, pl.BlockSpec(subkeys.shape, lambda i, j: (0, 0, 0)),
                  pl.BlockSpec((tn, d), lambda i, j: (j, 0))],
        out_specs=[pl.BlockSpec((tm // 2, PAIR_ROWS, PEER_KEYS), lambda i, j: (i, j, 0)), sel, sel, sel],
        out_shape=[jax.ShapeDtypeStruct((n // 2, e // PEER_KEYS, PEER_KEYS), jnp.uint32),
                   jax.ShapeDtypeStruct((n, PEER_SEL), jnp.int32), jax.ShapeDtypeStruct((n, PEER_SEL), jnp.int32),
                   jax.ShapeDtypeStruct((n, PEER_SEL), F32)],
        scratch_shapes=[pltpu.VMEM((tm, nq), F32), pltpu.VMEM((PEER_SEL, TOPK_LANES), F32),
                        pltpu.VMEM((PEER_SEL, TOPK_LANES), F32), pltpu.VMEM((PEER_SEL, TOPK_LANES), F32)],
        compiler_params=_cparams(("parallel", "arbitrary")),
        name="peer_score",
    )(h_hi, h_lo, wq_hi, wq_lo, subkeys, w_up_bf)


def _peer_select_kernel(s_ref, ia_ref, ib_ref, g_ref, a_ref, *, tn, unroll):
    NK = PEER_KEYS
    sub = lax.broadcasted_iota(jnp.int32, (NK, PEER_SEL), 0)

    def group(gidx, carry):
        pairs = [gidx * unroll + u for u in range(unroll)]
        toks = [2 * p + par for p in pairs for par in range(2)]
        is_a = [sub == ia_ref[pl.ds(t, 1), :] for t in toks]
        onehot_b = [jnp.where(sub == ib_ref[pl.ds(t, 1), :], 1.0, 0.0).astype(BF16) for t in toks]
        vals = []
        for u, p in enumerate(pairs):
            words = s_ref[p]
            for par in range(2):
                k = 2 * u + par
                x = pltpu.unpack_elementwise(words, index=par, packed_dtype=BF16, unpacked_dtype=F32)
                r = jnp.dot(x.astype(BF16), onehot_b[k], preferred_element_type=F32)
                vals.append(jnp.sum(jnp.where(is_a[k], r, 0.0), axis=0, keepdims=True))
        for u, p in enumerate(pairs):
            ys = []
            for par in range(2):
                k = 2 * u + par
                v = vals[k]
                act = g_ref[pl.ds(toks[k], 1), :] * (0.5 * v * (1.0 + lax.erf(v * (2.0 ** -0.5))))
                wa = jnp.where(is_a[k], act, 0.0).astype(BF16)
                ys.append(lax.dot_general(wa, onehot_b[k], (((1,), (1,)), ((), ())), preferred_element_type=F32))
            a_ref[p] = lax.bitcast_convert_type(pltpu.pack_elementwise(ys, packed_dtype=BF16), jnp.uint32)
        return carry

    lax.fori_loop(0, tn // 2 // unroll, group, 0)


def _peer_select(s, ia, ib, gate, tn=128, unroll=8):
    n = ia.shape[0]
    big = pl.BlockSpec((tn // 2, PEER_KEYS, PEER_KEYS), lambda i: (i, 0, 0))
    small = pl.BlockSpec((tn, PEER_SEL), lambda i: (i, 0))
    return pl.pallas_call(
        functools.partial(_peer_select_kernel, tn=tn, unroll=unroll),
        grid=(n // tn,),
        in_specs=[big, small, small, small],
        out_specs=big,
        out_shape=jax.ShapeDtypeStruct(s.shape, jnp.uint32),
        compiler_params=_cparams(("parallel",)),
        name="peer_select",
    )(s, ia, ib, gate)


def _peer_down_kernel(a_ref, w_ref, x_ref, g_ref, o_ref, acc_scr, *, final_norm):
    k = pl.program_id(1)

    @pl.when(k == 0)
    def _():
        acc_scr[...] = jnp.zeros(acc_scr.shape, F32)

    pairs = a_ref.shape[0] // DOWN_CHUNKS
    for r in range(DOWN_CHUNKS):
        lhs = _from_pair_major(a_ref[r * pairs:(r + 1) * pairs])
        acc_scr[2 * r * pairs:2 * (r + 1) * pairs, :] += jnp.dot(lhs, w_ref[...], preferred_element_type=F32)

    @pl.when(k == pl.num_programs(1) - 1)
    def _():
        x = x_ref[...] + acc_scr[...]
        if final_norm:
            x = x * lax.rsqrt(jnp.mean(x * x, axis=-1, keepdims=True) + EPS) * g_ref[...]
        o_ref[...] = x


def _peer_down(a, w_down_bf, x1, norm_g, final_norm, tm=1024):
    n = x1.shape[0]
    e, d = w_down_bf.shape
    tk = PAIR_ROWS * PEER_KEYS
    return pl.pallas_call(
        functools.partial(_peer_down_kernel, final_norm=final_norm),
        grid=(n // tm, e // tk),
        in_specs=[pl.BlockSpec((tm // 2, PAIR_ROWS, PEER_KEYS), lambda i, k: (i, k, 0)),
                  pl.BlockSpec((tk, d), lambda i, k: (k, 0)),
                  pl.BlockSpec((tm, d), lambda i, k: (i, 0)), pl.BlockSpec((1, d), lambda i, k: (0, 0))],
        out_specs=pl.BlockSpec((tm, d), lambda i, k: (i, 0)),
        out_shape=jax.ShapeDtypeStruct((n, d), F32),
        scratch_shapes=[pltpu.VMEM((tm, d), F32)],
        compiler_params=_cparams(("parallel", "arbitrary")),
        name="peer_down",
    )(a, w_down_bf, x1, norm_g.reshape(1, d))


def _peer(x1, h2b, h2lo, w_query, subkeys, w_up, w_down, norm_g, final_norm):
    s, ia, ib, gate = _peer_score(h2b, h2lo, w_query, subkeys, w_up.astype(BF16))
    a = _peer_select(s, ia, ib, gate)
    return _peer_down(a, w_down.astype(BF16), x1, norm_g, final_norm)


def kernel(x_prompt, x_sample, state_delta, state_conv, cache_win0, cache_win1, cache_win2, state_pool, norm_mix, w_in, conv_w, a_log, dt_bias, dn_norm, w_pool, pool_scale, w_branch, w_out, norm_ffn, peer_query, peer_subkeys, peer_up, peer_down, norm_final):
    Bp, Tp, D = x_prompt.shape
    Bs, Ts, _ = x_sample.shape
    depth = w_in.shape[0]
    past = PAST_LEN
    npr = Bp * Tp
    caches = (cache_win0, cache_win1, cache_win2)
    caches_t = [jnp.transpose(c, (0, 1, 3, 4, 5, 2)) for c in caches]
    x = jnp.concatenate([x_prompt.reshape(npr, D), x_sample.reshape(Bs * Ts, D)], axis=0)
    cos_p, sin_p = _rope_tables(jnp.arange(Tp))
    cos_s, sin_s = _rope_tables(past + jnp.arange(Ts))
    nconv = CONV_W - 1
    dn_w = DN_HEADS * DN_DIM

    delta_p, delta_s, conv_p, conv_s, pool_p, pool_s = [], [], [], [], [], []
    win_p = [[] for _ in SWA_GROUPS]
    new_s = [[] for _ in SWA_GROUPS]
    for l in range(depth):
        P = _inproj(x, norm_mix[l], _pack_w_in(w_in[l]))
        Ps = P[npr:].reshape(Bs, Ts, P_COLS)
        cw8 = jnp.pad(conv_w[l], ((0, 8 - CONV_W), (0, 0)))

        def prompt_tail(rows, col, width):
            return jnp.stack([lax.slice(P, ((b + 1) * Tp - rows, col), ((b + 1) * Tp, col + width)) for b in range(Bp)])

        abT_p = jnp.transpose(lax.slice(P, (0, COL_AB), (npr, COL_AB + 8)).reshape(Bp, Tp, 8), (0, 2, 1))
        abT_s = jnp.transpose(Ps[:, :, COL_AB:COL_AB + 8], (0, 2, 1))
        oa_p, dp = _deltanet(P, abT_p, jnp.zeros((Bp, 8, 3 * dn_w), F32), cw8,
                              jnp.zeros((Bp, DN_HEADS, DN_DIM, DN_DIM), F32), a_log[l], dt_bias[l], dn_norm[l],
                              0, Bp, Tp, 4 * DN_CHUNK, DN_CHUNK)
        oa_s, ds = _deltanet(P, abT_s, jnp.pad(state_conv[l], ((0, 0), (8 - nconv, 0), (0, 0))), cw8,
                              state_delta[l], a_log[l], dt_bias[l], dn_norm[l], npr, Bs, Ts, Ts, min(DN_CHUNK, Ts), nseq=8)
        delta_p.append(dp)
        delta_s.append(ds)
        conv_p.append(prompt_tail(nconv, COL_QKV_A, 3 * dn_w))
        conv_s.append(jnp.concatenate([state_conv[l], Ps[:, :, COL_QKV_A:COL_QKV_A + 3 * dn_w]], axis=1)[:, -nconv:])

        attn_p, attn_s = [], []
        for gi, (window, dil) in enumerate(SWA_GROUPS):
            o_p, ml_p, kr_p = _attn_prompt(P, cos_p, sin_p, gi, Bp, Tp)
            o_s, ml_s, kr_s = _attn_sample(P, cos_s, sin_s, caches_t[gi], l, gi, npr, Bs, Ts)
            attn_p += [o_p, ml_p]
            attn_s += [o_s, ml_s]
            vcol = _qkvb_col(gi, 2)
            keep = min(window, Tp)
            kv_p = jnp.stack([kr_p.reshape(Bp, Tp, SWA_OUT)[:, Tp - keep:], prompt_tail(keep, vcol, SWA_OUT)], axis=2)
            win_p[gi].append(kv_p.reshape(Bp, keep, 2, SWA_HEADS, SWA_DIM))
            kv_s = jnp.stack([kr_s.reshape(Bs, Ts, SWA_OUT), Ps[:, :, vcol:vcol + SWA_OUT]], axis=2)
            new_s[gi].append(kv_s.reshape(Bs, Ts, 2, SWA_HEADS, SWA_DIM))

        wp_bf = w_pool[l].astype(BF16)
        oc_p = _pool(P, jnp.zeros((Bp, POOL_BUF + 1, POOL_WIDTH), F32), wp_bf, pool_scale[l], 0, Bp, Tp, 256, 0)
        oc_s = _pool(P, jnp.pad(state_pool[l], ((0, 0), (1, 0), (0, 0))), wp_bf, pool_scale[l], npr, Bs, Ts, Ts, past)
        pool_p.append(prompt_tail(POOL_BUF, COL_U, POOL_WIDTH))
        pool_s.append(jnp.concatenate([state_pool[l], Ps[:, :, COL_U:COL_U + POOL_WIDTH]], axis=1)[:, -POOL_BUF:])

        wb = w_branch[l].astype(BF16)
        x1, h2b, h2lo = _merge(x, P, [oa_p] + attn_p + [oc_p], [oa_s] + attn_s + [oc_s],
                               wb[:dn_w], wb[dn_w:dn_w + SWA_OUT], wb[dn_w + SWA_OUT:], w_out[l].astype(BF16), norm_ffn[l])
        x = _peer(x1, h2b, h2lo, peer_query[l], peer_subkeys[l], peer_up[l], peer_down[l], norm_final, l == depth - 1)

    y_p = x[:npr].reshape(Bp, Tp, D)
    y_s = x[npr:].reshape(Bs, Ts, D)
    st = jnp.stack
    win_s = []
    for gi, (window, dil) in enumerate(SWA_GROUPS):
        kv_all = jnp.concatenate([caches[gi], st(new_s[gi])], axis=2)
        win_s.append(kv_all[:, :, kv_all.shape[2] - min(window, kv_all.shape[2]):])
    return (y_p, y_s, st(delta_p), st(delta_s), st(conv_p), st(conv_s),
            st(win_p[0]), win_s[0], st(win_p[1]), win_s[1], st(win_p[2]), win_s[2], st(pool_p), st(pool_s))
```

```python
import functools

import jax
import jax.numpy as jnp
from jax import lax
from jax.experimental import pallas as pl
from jax.experimental.pallas import tpu as pltpu

F32 = jnp.float32
BF16 = jnp.bfloat16
HI = lax.Precision.HIGHEST
EPS = 1e-6
NEG_INF = float("-inf")

D_MODEL = 1024
DN_HEADS = 4
DN_DIM = 128
DN_CHUNK = 64
CONV_W = 4
SWA_GROUPS = ((128, 1), (512, 4), (2048, 16))
SWA_HEADS = 4
SWA_DIM = 64
SWA_OUT = SWA_HEADS * SWA_DIM
SWA_BLOCK = 128
ROPE_THETA = 10000.0
POOL_WINDOWS = (2, 4, 8, 16)
POOL_GROUP = 384
POOL_WIDTH = 1536
POOL_BUF = 15
PEER_KEYS = 128
PEER_HEADS = 8
PEER_TOPK = 16
PEER_SEL = PEER_HEADS * PEER_TOPK
PAST_LEN = 2048

COL_QKV_A = 0
COL_U = 1536
COL_GATE = 3072
COL_Z = 6144
COL_QKV_B = 6656
COL_AB = 8960
P_COLS = 9216
VMEM_LIMIT = 56 * 1024 * 1024


def _cparams(sem):
    return pltpu.CompilerParams(dimension_semantics=sem, vmem_limit_bytes=VMEM_LIMIT)


def _sigmoid(x):
    return 1.0 / (1.0 + jnp.exp(-x))


def _silu(x):
    return x * _sigmoid(x)


def _inproj_kernel(x_ref, g_ref, w_ref, o_ref, h_scr):
    @pl.when(pl.program_id(1) == 0)
    def _():
        x = x_ref[...]
        y = x * lax.rsqrt(jnp.mean(x * x, axis=-1, keepdims=True) + EPS)
        h_scr[...] = (y * g_ref[...]).astype(BF16)

    o_ref[...] = jnp.dot(h_scr[...], w_ref[...], preferred_element_type=F32)


def _inproj(x, g, w_bf, tm=1024, tn=2304):
    n, d = x.shape
    nc = w_bf.shape[1]
    return pl.pallas_call(
        _inproj_kernel,
        grid=(n // tm, nc // tn),
        in_specs=[pl.BlockSpec((tm, d), lambda i, j: (i, 0)),
                  pl.BlockSpec((1, d), lambda i, j: (0, 0)),
                  pl.BlockSpec((d, tn), lambda i, j: (0, j))],
        out_specs=pl.BlockSpec((tm, tn), lambda i, j: (i, j)),
        out_shape=jax.ShapeDtypeStruct((n, nc), F32),
        scratch_shapes=[pltpu.VMEM((tm, d), BF16)],
        compiler_params=_cparams(("parallel", "arbitrary")),
        name="inproj",
    )(x, g.reshape(1, d), w_bf)


def _pack_w_in(w):
    d = w.shape[0]
    qkv_b = w[:, 2056:4360].reshape(d, 3, len(SWA_GROUPS), SWA_OUT).transpose(0, 2, 1, 3).reshape(d, 2304)
    parts = [w[:, 0:1536], w[:, 4360:5896], w[:, 5896:8968], w[:, 1536:2048], qkv_b, w[:, 2048:2056],
             jnp.zeros((d, P_COLS - 8968), w.dtype)]
    return jnp.concatenate(parts, axis=1).astype(BF16)


def _qkvb_col(gi, which):
    return COL_QKV_B + (gi * 3 + which) * SWA_OUT


def _dn_kernel(alog_ref, dtb_ref, q_ref, k_ref, v_ref, z_ref, ab_ref, abT_ref, c0_ref, cw_ref, s0_ref, dnn_ref,
                o_ref, sout_ref, S_scr, ext_scr, *, C, Tc, nseq):
    H, DH = DN_HEADS, DN_DIM
    W = H * DH
    t = pl.program_id(1)

    @pl.when(t == 0)
    def _():
        S_scr[...] = s0_ref[...]
        ext_scr[:, 0:8, :] = c0_ref[...]

    cw = cw_ref[...]
    acts = []
    for sq in range(nseq):
        rs = slice(sq * Tc, (sq + 1) * Tc)
        ext_scr[sq, 8:8 + Tc, 0:W] = q_ref[rs, :]
        ext_scr[sq, 8:8 + Tc, W:2 * W] = k_ref[rs, :]
        ext_scr[sq, 8:8 + Tc, 2 * W:3 * W] = v_ref[rs, :]
        acts.append(_silu(ext_scr[sq, 5:5 + Tc, :] * cw[0:1] + ext_scr[sq, 6:6 + Tc, :] * cw[1:2]
                          + ext_scr[sq, 7:7 + Tc, :] * cw[2:3] + ext_scr[sq, 8:8 + Tc, :] * cw[3:4]))
        ext_scr[sq, 0:8, :] = ext_scr[sq, Tc:Tc + 8, :]

    def l2n(x):
        return x * lax.rsqrt(jnp.sum(x * x, axis=-1, keepdims=True) + EPS)

    def softplus(x):
        return jnp.maximum(x, 0.0) + jnp.log1p(jnp.exp(-jnp.abs(x)))

    def split(a):
        hi = a.astype(BF16)
        return hi, (a - hi.astype(F32)).astype(BF16)

    def mm3(a, b, dims):
        (ah, al), (bh, bl) = split(a), split(b)

        def dg(x, y):
            return lax.dot_general(x, y, (dims, ((), ())), preferred_element_type=F32)

        return dg(ah, bh) + dg(ah, bl) + dg(al, bh)

    def mm(a, b):
        return mm3(a, b, ((1,), (0,)))

    def mm_nt(a, b):
        return mm3(a, b, ((1,), (1,)))

    def mm_tn(a, b):
        return mm3(a, b, ((0,), (0,)))

    ab_all = ab_ref[...]
    zs = _silu(z_ref[...])
    dnn = dnn_ref[...]
    ii = lax.broadcasted_iota(jnp.int32, (C, C), 0)
    jj = lax.broadcasted_iota(jnp.int32, (C, C), 1)
    incl = ii >= jj
    strict = ii > jj
    eye = jnp.where(ii == jj, 1.0, 0.0).astype(F32)
    nck = Tc // C

    pairs = []
    for sq in range(nseq):
        act = acts[sq]
        ab = ab_all[sq * Tc:(sq + 1) * Tc]
        abT = abT_ref[sq]
        for h in range(H):
            qh = l2n(act[:, h * DH:(h + 1) * DH]) * (DH ** -0.5)
            kh = l2n(act[:, W + h * DH:W + (h + 1) * DH])
            vh = act[:, 2 * W + h * DH:2 * W + (h + 1) * DH]
            neg_a = -jnp.exp(jnp.full((1, 1), alog_ref[h], F32))
            g_col_h = neg_a * softplus(ab[:, h:h + 1] + dtb_ref[h])
            g_row_h = neg_a * softplus(abT[h:h + 1, :] + dtb_ref[h])
            beta_h = _sigmoid(ab[:, H + h:H + h + 1])
            for c in range(nck):
                sl = slice(c * C, (c + 1) * C)
                qc, kc, vc = qh[sl], kh[sl], vh[sl]
                g_col, g_row, beta = g_col_h[sl], g_row_h[:, sl], beta_h[sl]
                gc_col = jnp.sum(jnp.where(incl, g_row, 0.0), axis=1, keepdims=True)
                gc_row = jnp.sum(jnp.where(ii <= jj, g_col, 0.0), axis=0, keepdims=True)
                gc_last = jnp.sum(g_row, axis=1, keepdims=True)
                decay = jnp.where(incl, jnp.exp(jnp.where(incl, gc_col - gc_row, 0.0)), 0.0)
                kb = kc * beta
                egc = jnp.exp(gc_col)
                pairs.append(dict(st=(sq, h), c=c, rows=slice(sq * Tc + c * C, sq * Tc + (c + 1) * C),
                                  qc=qc, kc=kc, kb=kb, vb=vc * beta, decay=decay, egc=egc,
                                  kdec=kc * jnp.exp(gc_last - gc_col), glast=jnp.exp(gc_last)))
    for p in pairs:
        p["pw"] = -(mm_nt(p["kb"], p["kc"]) * jnp.where(strict, p["decay"], 0.0))
        p["intra"] = mm_nt(p["qc"], p["kc"]) * p["decay"]
    for p in pairs:
        p["inv"] = eye + p["pw"]
    m = 2
    while m < C:
        for p in pairs:
            p["pw"] = mm(p["pw"], p["pw"])
        for p in pairs:
            p["inv"] = p["inv"] + mm(p["inv"], p["pw"])
        m *= 2
    for p in pairs:
        p["u"] = mm(p["inv"], p["vb"])
        p["w"] = mm(p["inv"], p["kb"] * p["egc"])
    S = {(sq, h): S_scr[sq, h] for sq in range(nseq) for h in range(H)}
    for c in range(nck):
        cur = [p for p in pairs if p["c"] == c]
        vnew = [p["u"] - mm(p["w"], S[p["st"]]) for p in cur]
        for p, vn in zip(cur, vnew):
            st = p["st"]
            cols = slice(st[1] * DH, (st[1] + 1) * DH)
            o = mm(p["qc"] * p["egc"], S[st]) + mm(p["intra"], vn)
            S[st] = S[st] * p["glast"] + mm_tn(p["kdec"], vn)
            o = o * lax.rsqrt(jnp.mean(o * o, axis=-1, keepdims=True) + EPS) * dnn
            o_ref[p["rows"], cols] = o * zs[p["rows"], cols]
    for (sq, h), val in S.items():
        S_scr[sq, h] = val

    @pl.when(t == pl.num_programs(1) - 1)
    def _():
        for (sq, h), val in S.items():
            sout_ref[sq, h] = val


def _deltanet(P, abT, conv0p, cw8, state0, a_log, dt_bias, dn_norm, row0, B, T, Tc, C, nseq=1):
    nt = T // Tc
    assert nseq == 1 or nt == 1
    rt = nseq * Tc
    rb0 = row0 // rt
    H, W = DN_HEADS, DN_HEADS * DN_DIM

    def rows(width, col):
        return pl.BlockSpec((rt, width), lambda b, t, c=col: (rb0 + b * nt + t, c))

    smem = pl.BlockSpec(memory_space=pltpu.SMEM)
    return pl.pallas_call(
        functools.partial(_dn_kernel, C=C, Tc=Tc, nseq=nseq),
        grid=(B // nseq, nt),
        in_specs=[smem, smem, rows(W, 0), rows(W, 1), rows(W, 2), rows(W, COL_Z // W), rows(128, COL_AB // 128),
                  pl.BlockSpec((nseq, 8, Tc), lambda b, t: (b, 0, t)),
                  pl.BlockSpec((nseq, 8, 3 * W), lambda b, t: (b, 0, 0)),
                  pl.BlockSpec((8, 3 * W), lambda b, t: (0, 0)),
                  pl.BlockSpec((nseq, H, DN_DIM, DN_DIM), lambda b, t: (b, 0, 0, 0)),
                  pl.BlockSpec((1, DN_DIM), lambda b, t: (0, 0))],
        out_specs=[pl.BlockSpec((rt, W), lambda b, t: (b * nt + t, 0)),
                   pl.BlockSpec((nseq, H, DN_DIM, DN_DIM), lambda b, t: (b, 0, 0, 0))],
        out_shape=[jax.ShapeDtypeStruct((B * T, W), F32), jax.ShapeDtypeStruct((B, H, DN_DIM, DN_DIM), F32)],
        scratch_shapes=[pltpu.VMEM((nseq, H, DN_DIM, DN_DIM), F32), pltpu.VMEM((nseq, Tc + 8, 3 * W), F32)],
        compiler_params=_cparams(("parallel", "arbitrary")),
        name="deltanet",
    )(a_log, dt_bias, P, P, P, P, P, abT, conv0p, cw8, state0, dn_norm.reshape(1, DN_DIM))


def _rope_tables(pos):
    half = SWA_DIM // 2
    inv = ROPE_THETA ** (-jnp.arange(half, dtype=F32) / half)
    ang = pos.astype(F32)[:, None] * inv[None, :]
    cos = jnp.tile(jnp.cos(ang), (1, 2 * SWA_HEADS))
    sin = jnp.tile(jnp.sin(ang), (1, 2 * SWA_HEADS))
    return cos, sin


def _rope(x, cos, sin):
    lane = lax.broadcasted_iota(jnp.int32, x.shape, 1)
    first = (lane % SWA_DIM) < (SWA_DIM // 2)
    w = x.shape[1]
    xr = jnp.where(first, -pltpu.roll(x, w - SWA_DIM // 2, 1), pltpu.roll(x, SWA_DIM // 2, 1))
    return x * cos + xr * sin


def _mm_nt(a, b):
    return lax.dot_general(a, b, (((1,), (1,)), ((), ())), precision=HI, preferred_element_type=F32)


def _low_half(shape):
    return (lax.broadcasted_iota(jnp.int32, shape, 1) % SWA_DIM) < (SWA_DIM // 2)


def _pack_ml(m_full, l_full):
    return jnp.where(_low_half(m_full.shape), m_full, l_full)


def _unpack_ml(ml):
    low = _low_half(ml.shape)
    w = ml.shape[1]
    m = jnp.where(low, ml, pltpu.roll(ml, SWA_DIM // 2, 1))
    l = jnp.where(low, pltpu.roll(ml, w - SWA_DIM // 2, 1), ml)
    return m, l


def _attn_prompt_kernel(q_ref, kc_ref, kp_ref, vc_ref, vp_ref, cq_ref, sq_ref, cp_ref, sp_ref,
                        o_ref, ml_ref, kr_ref, *, n_back, nq):
    i = pl.program_id(2)
    Bq = SWA_BLOCK
    H = SWA_HEADS
    q_all = _rope(q_ref[...], cq_ref[...], sq_ref[...])
    k_cur = _rope(kc_ref[...], cq_ref[...], sq_ref[...])
    k_prev = _rope(kp_ref[...], cp_ref[...], sp_ref[...])
    kr_ref[...] = k_cur
    k_all = jnp.concatenate([k_prev, k_cur], axis=0).astype(BF16)
    v_all = jnp.concatenate([vp_ref[...], vc_ref[...]], axis=0).astype(BF16)
    qi = lax.broadcasted_iota(jnp.int32, (H * Bq, 2 * Bq), 0) % Bq
    kj = lax.broadcasted_iota(jnp.int32, (H * Bq, 2 * Bq), 1)
    rel = Bq + qi - kj
    in_band = (rel >= 0) & (rel <= n_back)
    head = lax.broadcasted_iota(jnp.int32, (Bq, SWA_OUT), 1) // SWA_DIM
    blocks = range(nq)
    scores = []
    for u in blocks:
        q = q_all[u * Bq:(u + 1) * Bq]
        qs = jnp.concatenate([jnp.where(head == h, q, 0.0) for h in range(H)], axis=0).astype(BF16)
        scores.append(lax.dot_general(qs, k_all[u * Bq:(u + 2) * Bq], (((1,), (1,)), ((), ())),
                                      preferred_element_type=F32))
    stats = []
    for u in blocks:
        valid = in_band & ((i * nq + u) * Bq + qi - rel >= 0)
        s = jnp.where(valid, scores[u] * (SWA_DIM ** -0.5), NEG_INF)
        m = jnp.max(s, axis=1, keepdims=True)
        p = jnp.exp(s - m)
        stats.append((m, jnp.sum(p, axis=1, keepdims=True), p.astype(BF16)))
    accs = [jnp.dot(stats[u][2], v_all[u * Bq:(u + 2) * Bq], preferred_element_type=F32) for u in blocks]
    for u in blocks:
        m, l, _ = stats[u]
        o = jnp.zeros((Bq, SWA_OUT), F32)
        m_full = jnp.zeros((Bq, SWA_OUT), F32)
        l_full = jnp.zeros((Bq, SWA_OUT), F32)
        for h in range(H):
            rows = slice(h * Bq, (h + 1) * Bq)
            o = jnp.where(head == h, accs[u][rows], o)
            m_full = jnp.where(head == h, m[rows], m_full)
            l_full = jnp.where(head == h, l[rows], l_full)
        out_rows = slice(u * Bq, (u + 1) * Bq)
        o_ref[out_rows, :] = o
        ml_ref[out_rows, :] = _pack_ml(m_full, l_full)


def _attn_prompt(P, cos, sin, gi, B, T):
    window, dil = SWA_GROUPS[gi]
    n_back = window // dil
    assert n_back <= SWA_BLOCK
    L = T // dil
    nb = L // SWA_BLOCK
    if dil == 1:
        Pd, cw, c0 = P, 0, _qkvb_col(gi, 0) // SWA_OUT
    else:
        Pd = P[:B * T, _qkvb_col(gi, 0):_qkvb_col(gi, 0) + 3 * SWA_OUT].reshape(B * L, dil * 3 * SWA_OUT)
        cw, c0 = 3, 0
    cd = cos.reshape(L, dil * SWA_OUT)
    sd = sin.reshape(L, dil * SWA_OUT)
    nq = 4 if nb % 4 == 0 else 2 if nb % 2 == 0 else 1
    ns = nb // nq
    blk = (nq * SWA_BLOCK, SWA_OUT)
    pblk = (SWA_BLOCK, SWA_OUT)

    def cur(which):
        return pl.BlockSpec(blk, lambda b, p, i, c=c0 + which: (b * ns + i, p * cw + c))

    def prev(which):
        return pl.BlockSpec(pblk, lambda b, p, i, c=c0 + which: (b * nb + jnp.maximum(i * nq - 1, 0), p * cw + c))

    tcur = pl.BlockSpec(blk, lambda b, p, i: (i, p))
    tprev = pl.BlockSpec(pblk, lambda b, p, i: (jnp.maximum(i * nq - 1, 0), p))
    ospec = pl.BlockSpec(blk, lambda b, p, i: (b * ns + i, p))
    oshape = jax.ShapeDtypeStruct((B * L, dil * SWA_OUT), F32)
    outs = pl.pallas_call(
        functools.partial(_attn_prompt_kernel, n_back=n_back, nq=nq),
        grid=(B, dil, ns),
        in_specs=[cur(0), cur(1), prev(1), cur(2), prev(2), tcur, tcur, tprev, tprev],
        out_specs=[ospec] * 3,
        out_shape=[oshape] * 3,
        compiler_params=_cparams(("parallel", "parallel", "arbitrary")),
        name=f"attn_prompt{gi}",
    )(Pd, Pd, Pd, Pd, Pd, cd, sd, cd, sd)
    return [a.reshape(B * T, SWA_OUT) for a in outs]


def _attn_sample_kernel(q_ref, k_ref, v_ref, cos_ref, sin_ref, kv_ref, o_ref, ml_ref, kr_ref, *, dil, T):
    nseq = q_ref.shape[0] // T
    W = kv_ref.shape[-1]
    H = SWA_HEADS
    cos = jnp.concatenate([cos_ref[...]] * nseq, axis=0)
    sin = jnp.concatenate([sin_ref[...]] * nseq, axis=0)
    q_all = _rope(q_ref[...], cos, sin)
    k_all = _rope(k_ref[...], cos, sin)
    v_all = v_ref[...]
    kr_ref[...] = k_all
    head = lax.broadcasted_iota(jnp.int32, (T, SWA_OUT), 1) // SWA_DIM
    tq = lax.broadcasted_iota(jnp.int32, (H * T, W), 0) % T
    wk = lax.broadcasted_iota(jnp.int32, (H * T, W), 1)
    valid_buf = (((W + tq - wk) & (dil - 1)) == 0) & (wk >= tq)
    tq2 = lax.broadcasted_iota(jnp.int32, (H * T, T), 0) % T
    tk2 = lax.broadcasted_iota(jnp.int32, (H * T, T), 1)
    valid_new = (((tq2 - tk2) & (dil - 1)) == 0) & (tk2 <= tq2)
    scale = SWA_DIM ** -0.5
    nt_dims = (((1,), (1,)), ((), ()))
    seqs = range(nseq)
    raw = []
    for sq in seqs:
        rs = slice(sq * T, (sq + 1) * T)
        qs = jnp.concatenate([jnp.where(head == h, q_all[rs], 0.0) for h in range(H)], axis=0).astype(BF16)
        k_t = kv_ref[0, sq, 0].reshape(H * SWA_DIM, W).astype(BF16)
        raw.append((lax.dot_general(qs, k_all[rs].astype(BF16), nt_dims, preferred_element_type=F32),
                    jnp.dot(qs, k_t, preferred_element_type=F32)))
    stats = []
    for sq in seqs:
        sn = jnp.where(valid_new, raw[sq][0] * scale, NEG_INF)
        sb = jnp.where(valid_buf, raw[sq][1] * scale, NEG_INF)
        m = jnp.maximum(jnp.max(sn, axis=1, keepdims=True), jnp.max(sb, axis=1, keepdims=True))
        pn = jnp.exp(sn - m)
        pb = jnp.exp(sb - m)
        stats.append((m, jnp.sum(pn, axis=1, keepdims=True) + jnp.sum(pb, axis=1, keepdims=True),
                      pn.astype(BF16), pb.astype(BF16)))
    accs = []
    for sq in seqs:
        rs = slice(sq * T, (sq + 1) * T)
        v_t = kv_ref[0, sq, 1].reshape(H * SWA_DIM, W).astype(BF16)
        accs.append(jnp.dot(stats[sq][2], v_all[rs].astype(BF16), preferred_element_type=F32)
                    + lax.dot_general(stats[sq][3], v_t, nt_dims, preferred_element_type=F32))
    for sq in seqs:
        m, l = stats[sq][0], stats[sq][1]
        o = jnp.zeros((T, SWA_OUT), F32)
        m_full = jnp.zeros((T, SWA_OUT), F32)
        l_full = jnp.zeros((T, SWA_OUT), F32)
        for h in range(H):
            rows = slice(h * T, (h + 1) * T)
            o = jnp.where(head == h, accs[sq][rows], o)
            m_full = jnp.where(head == h, m[rows], m_full)
            l_full = jnp.where(head == h, l[rows], l_full)
        rs = slice(sq * T, (sq + 1) * T)
        o_ref[rs, :] = o
        ml_ref[rs, :] = _pack_ml(m_full, l_full)


def _attn_sample(P, cos, sin, cache_t, layer, gi, row0, B, T):
    window, dil = SWA_GROUPS[gi]
    W = cache_t.shape[-1]
    nseq = 2 if W > 512 else 4
    assert W == window and W % dil == 0 and dil & (dil - 1) == 0 and row0 % (nseq * T) == 0 and B % nseq == 0
    rb0 = row0 // (nseq * T)
    blk = (nseq * T, SWA_OUT)

    def rows(which):
        return pl.BlockSpec(blk, lambda b, c=_qkvb_col(gi, which) // SWA_OUT: (rb0 + b, c))

    tab = pl.BlockSpec((T, SWA_OUT), lambda b: (0, 0))
    ospec = pl.BlockSpec(blk, lambda b: (b, 0))
    oshape = jax.ShapeDtypeStruct((B * T, SWA_OUT), F32)
    return pl.pallas_call(
        functools.partial(_attn_sample_kernel, dil=dil, T=T),
        grid=(B // nseq,),
        in_specs=[rows(0), rows(1), rows(2), tab, tab,
                  pl.BlockSpec((1, nseq, 2, SWA_HEADS, SWA_DIM, W), lambda b: (layer, b, 0, 0, 0, 0))],
        out_specs=[ospec] * 3,
        out_shape=[oshape] * 3,
        compiler_params=_cparams(("parallel",)),
        name=f"attn_sample{gi}",
    )(P, P, P, cos, sin, cache_t)


def _pool_kernel(u_ref, pre_ref, w_ref, sc_ref, o_ref, ext_scr, *, Tc, pos0):
    t = pl.program_id(1)
    nb = POOL_BUF + 1

    @pl.when(t == 0)
    def _():
        ext_scr[0:nb, :] = pre_ref[0]

    u = u_ref[...]
    ext_scr[nb:nb + Tc, :] = u
    pos = (pos0 + t * Tc + lax.broadcasted_iota(jnp.int32, (Tc, 1), 0)).astype(F32)
    for gi, w in enumerate(POOL_WINDOWS):
        lo, hi = gi * POOL_GROUP, (gi + 1) * POOL_GROUP
        ug = u[:, lo:hi]
        acc = ug
        for i in range(1, w):
            acc = acc + ext_scr[nb - i:nb - i + Tc, lo:hi]
        d = acc / jnp.minimum(float(w), pos + 1.0) - ug
        y = jnp.dot(d.astype(BF16), w_ref[gi], preferred_element_type=F32)
        o_ref[:, lo:hi] = y * sc_ref[:, lo:hi]
    ext_scr[0:nb, :] = ext_scr[Tc:Tc + nb, :]


def _pool(P, prefix16, w_pool_bf, pool_scale, row0, B, T, Tc, pos0):
    nt = T // Tc
    rb0 = row0 // Tc
    return pl.pallas_call(
        functools.partial(_pool_kernel, Tc=Tc, pos0=pos0),
        grid=(B, nt),
        in_specs=[pl.BlockSpec((Tc, POOL_WIDTH), lambda b, t: (rb0 + b * nt + t, COL_U // POOL_WIDTH)),
                  pl.BlockSpec((1, POOL_BUF + 1, POOL_WIDTH), lambda b, t: (b, 0, 0)),
                  pl.BlockSpec((len(POOL_WINDOWS), POOL_GROUP, POOL_GROUP), lambda b, t: (0, 0, 0)),
                  pl.BlockSpec((1, POOL_WIDTH), lambda b, t: (0, 0))],
        out_specs=pl.BlockSpec((Tc, POOL_WIDTH), lambda b, t: (b * nt + t, 0)),
        out_shape=jax.ShapeDtypeStruct((B * T, POOL_WIDTH), F32),
        scratch_shapes=[pltpu.VMEM((Tc + POOL_BUF + 1, POOL_WIDTH), F32)],
        compiler_params=_cparams(("parallel", "arbitrary")),
        name="pool",
    )(P, prefix16, w_pool_bf, pool_scale.reshape(1, POOL_WIDTH))


N_MIX = 8


def _merge_kernel(x_ref, g0_ref, g1_ref, g2_ref, *refs, np_blocks):
    mix_p, mix_s = refs[:N_MIX], refs[N_MIX:2 * N_MIX]
    wa_ref, wb_ref, wc_ref, wo_ref, nf_ref, x1_ref, h2b_ref, h2lo_ref = refs[2 * N_MIX:]
    is_p = pl.program_id(0) < np_blocks
    oa, o0, ml0, o1, ml1, o2, ml2, oc = [jnp.where(is_p, p[...], s[...]) for p, s in zip(mix_p, mix_s)]
    (m0, l0), (m1, l1), (m2, l2) = _unpack_ml(ml0), _unpack_ml(ml1), _unpack_ml(ml2)
    mx = jnp.maximum(jnp.maximum(m0, m1), m2)
    e0, e1, e2 = jnp.exp(m0 - mx), jnp.exp(m1 - mx), jnp.exp(m2 - mx)
    ob = (e0 * o0 + e1 * o1 + e2 * o2) / (e0 * l0 + e1 * l1 + e2 * l2)

    def proj(a, w_ref):
        return jnp.dot(a.astype(BF16), w_ref[...], preferred_element_type=F32)

    merged = (_sigmoid(g0_ref[...]) * proj(oa, wa_ref) + _sigmoid(g1_ref[...]) * proj(ob, wb_ref)
              + _sigmoid(g2_ref[...]) * proj(oc, wc_ref))
    x1 = x_ref[...] + proj(merged, wo_ref)
    x1_ref[...] = x1
    h2 = x1 * lax.rsqrt(jnp.mean(x1 * x1, axis=-1, keepdims=True) + EPS) * nf_ref[...]
    h2b = h2.astype(BF16)
    h2b_ref[...] = h2b
    h2lo_ref[...] = (h2 - h2b.astype(F32)).astype(BF16)


def _merge(x, P, mix_p, mix_s, wa, wb, wc, wo, norm_ffn, tm=256):
    n, d = x.shape
    npb = mix_p[0].shape[0] // tm
    nsb = mix_s[0].shape[0] // tm
    assert npb + nsb == n // tm

    def rows(width, col=0):
        return pl.BlockSpec((tm, width), lambda i, c=col: (i, c))

    def prow(a):
        return pl.BlockSpec((tm, a.shape[1]), lambda i: (jnp.minimum(i, npb - 1), 0))

    def srow(a):
        return pl.BlockSpec((tm, a.shape[1]), lambda i: (jnp.maximum(i - npb, 0), 0))

    def full(a):
        return pl.BlockSpec(a.shape, lambda i: (0,) * a.ndim)

    nf = norm_ffn.reshape(1, d)
    g0 = COL_GATE // d
    return pl.pallas_call(
        functools.partial(_merge_kernel, np_blocks=npb),
        grid=(n // tm,),
        in_specs=[rows(d), rows(d, g0), rows(d, g0 + 1), rows(d, g0 + 2)] + [prow(a) for a in mix_p]
        + [srow(a) for a in mix_s] + [full(wa), full(wb), full(wc), full(wo), full(nf)],
        out_specs=[rows(d), rows(d), rows(d)],
        out_shape=[jax.ShapeDtypeStruct((n, d), F32), jax.ShapeDtypeStruct((n, d), BF16),
                   jax.ShapeDtypeStruct((n, d), BF16)],
        compiler_params=_cparams(("parallel",)),
        name="merge",
    )(x, P, P, P, *mix_p, *mix_s, wa, wb, wc, wo, nf)


def _topk_rows(s, iota, k):
    vals, idxs = [], []
    big = 1e9
    for _ in range(k):
        m = jnp.max(s, axis=0, keepdims=True)
        idx = jnp.min(jnp.where(s == m, iota, big), axis=0, keepdims=True)
        vals.append(m)
        idxs.append(idx)
        s = jnp.where(iota == idx, NEG_INF, s)
    return jnp.concatenate(vals, axis=0), jnp.concatenate(idxs, axis=0)


TOPK_LANES = 128
TOPK_HEADS_PER_STEP = 4


def _query_3pass(hh_ref, hl_ref, wqh_ref, wql_ref):
    return (jnp.dot(hh_ref[...], wqh_ref[...], preferred_element_type=F32)
            + jnp.dot(hh_ref[...], wql_ref[...], preferred_element_type=F32)
            + jnp.dot(hl_ref[...], wqh_ref[...], preferred_element_type=F32))


def _topk_consts():
    TK, CH = PEER_TOPK, TOPK_LANES
    kio = lax.broadcasted_iota(jnp.int32, (PEER_KEYS, CH), 0).astype(F32)
    rho = lax.broadcasted_iota(jnp.int32, (TK + 8 * 8, CH), 0)
    mid = rho - TK
    cand_r = jnp.where(rho < TK, 0, jnp.where(mid < 56, (mid >> 3) + 1, mid - 56 + 8))
    cand_c = jnp.where(rho < TK, rho, jnp.where(mid < 56, mid & 7, 0))
    cand_ok = (cand_r + 1) * (cand_c + 1) <= TK
    cio = (cand_r * TK + cand_c).astype(F32)
    return kio, cand_ok, cio


def _subkey_scores(h, r0, q_scr, sk_ref):
    c0 = pl.multiple_of(h * 2 * PEER_KEYS, 2 * PEER_KEYS)
    q1 = q_scr[pl.ds(r0, TOPK_LANES), pl.ds(c0, PEER_KEYS)]
    q2 = q_scr[pl.ds(r0, TOPK_LANES), pl.ds(c0 + PEER_KEYS, PEER_KEYS)]
    return _mm_nt(sk_ref[0], q1), _mm_nt(sk_ref[1], q2)


def _topk_head(h, scores, consts, ia_scr, ib_scr, g_scr):
    TK, CH = PEER_TOPK, TOPK_LANES
    kio, cand_ok, cio = consts
    s1, i1 = _topk_rows(scores[0], kio, TK)
    s2, i2 = _topk_rows(scores[1], kio, TK)
    cand = jnp.concatenate([s1[0:1] + s2] + [s1[r:r + 1] + s2[0:8] for r in range(1, 8)]
                           + [s1[8:TK] + s2[0:1]], axis=0)
    best, ic = _topk_rows(jnp.where(cand_ok, cand, NEG_INF), cio, TK)
    rk = jnp.floor(ic * (1.0 / TK))
    ck = ic - rk * TK
    ia = jnp.zeros((TK, CH), F32)
    ib = jnp.zeros((TK, CH), F32)
    for r in range(TK):
        ia = jnp.where(rk == float(r), i1[r:r + 1], ia)
        ib = jnp.where(ck == float(r), i2[r:r + 1], ib)
    e = jnp.exp(best - best[0:1])
    gate = e / jnp.sum(e, axis=0, keepdims=True)
    o0 = pl.multiple_of(h * TK, TK)
    ia_scr[pl.ds(o0, TK), :] = ia
    ib_scr[pl.ds(o0, TK), :] = ib
    g_scr[pl.ds(o0, TK), :] = gate


def _topk_emit(r0, ia_scr, ib_scr, g_scr, ia_ref, ib_ref, g_ref):
    ia_ref[pl.ds(r0, TOPK_LANES), :] = ia_scr[...].T.astype(jnp.int32)
    ib_ref[pl.ds(r0, TOPK_LANES), :] = ib_scr[...].T.astype(jnp.int32)
    g_ref[pl.ds(r0, TOPK_LANES), :] = g_scr[...].T


PAIR_ROWS = 8
DOWN_CHUNKS = 4


def _to_pair_major(x):
    xb = x.astype(BF16)
    chunks = [pltpu.bitcast(xb[:, c * PEER_KEYS:(c + 1) * PEER_KEYS], jnp.uint32) for c in range(PAIR_ROWS)]
    return jnp.swapaxes(jnp.stack(chunks, axis=0), 0, 1)


def _from_pair_major(words):
    y = jnp.swapaxes(words, 0, 1)
    return jnp.concatenate([pltpu.bitcast(y[c], BF16) for c in range(PAIR_ROWS)], axis=1)


def _peer_score_kernel(hh_ref, hl_ref, wqh_ref, wql_ref, sk_ref, w_ref, s_ref, ia_ref, ib_ref, g_ref,
                       q_scr, ia_scr, ib_scr, g_scr):
    j = pl.program_id(1)
    groups = PEER_HEADS // TOPK_HEADS_PER_STEP

    @pl.when(j == 0)
    def _():
        q_scr[...] = _query_3pass(hh_ref, hl_ref, wqh_ref, wql_ref)

    r0 = pl.multiple_of((j // groups) * TOPK_LANES, TOPK_LANES)
    hg = j % groups
    heads = [hg * TOPK_HEADS_PER_STEP + u for u in range(TOPK_HEADS_PER_STEP)]
    scores = [_subkey_scores(h, r0, q_scr, sk_ref) for h in heads]
    consts = _topk_consts()
    wcols = w_ref.shape[0] // TOPK_HEADS_PER_STEP
    parts = []
    for u, (h, sc) in enumerate(zip(heads, scores)):
        parts.append(lax.dot_general(hh_ref[...], w_ref[u * wcols:(u + 1) * wcols, :], (((1,), (1,)), ((), ())),
                                     preferred_element_type=F32))
        _topk_head(h, sc, consts, ia_scr, ib_scr, g_scr)
    s_ref[...] = _to_pair_major(jnp.concatenate(parts, axis=1))

    @pl.when(hg == groups - 1)
    def _():
        _topk_emit(r0, ia_scr, ib_scr, g_scr, ia_ref, ib_ref, g_ref)


def _peer_score(h_hi, h_lo, w_query, subkeys, w_up_bf):
    n, d = h_hi.shape
    e = w_up_bf.shape[0]
    nq = w_query.shape[1]
    tn = PAIR_ROWS * PEER_KEYS
    steps = e // tn
    tm = (steps * TOPK_HEADS_PER_STEP // PEER_HEADS) * TOPK_LANES
    assert n % tm == 0 and steps * TOPK_HEADS_PER_STEP % PEER_HEADS == 0
    wq_hi = w_query.astype(BF16)
    wq_lo = (w_query - wq_hi.astype(F32)).astype(BF16)
    rows = pl.BlockSpec((tm, d), lambda i, j: (i, 0))
    wq_spec = pl.BlockSpec((d, nq), lambda i, j: (0, 0))
    sel = pl.BlockSpec((tm, PEER_SEL), lambda i, j: (i, 0))
    return pl.pallas_call(
        _peer_score_kernel,
        grid=(n // tm, steps),
        in_specs=[rows, rows, wq_spec, wq_spec, pl.BlockSpec(subkeys.shape, lambda i, j: (0, 0, 0)),
                  pl.BlockSpec((tn, d), lambda i, j: (j, 0))],
        out_specs=[pl.BlockSpec((tm // 2, PAIR_ROWS, PEER_KEYS), lambda i, j: (i, j, 0)), sel, sel, sel],
        out_shape=[jax.ShapeDtypeStruct((n // 2, e // PEER_KEYS, PEER_KEYS), jnp.uint32),
                   jax.ShapeDtypeStruct((n, PEER_SEL), jnp.int32), jax.ShapeDtypeStruct((n, PEER_SEL), jnp.int32),
                   jax.ShapeDtypeStruct((n, PEER_SEL), F32)],
        scratch_shapes=[pltpu.VMEM((tm, nq), F32), pltpu.VMEM((PEER_SEL, TOPK_LANES), F32),
                        pltpu.VMEM((PEER_SEL, TOPK_LANES), F32), pltpu.VMEM((PEER_SEL, TOPK_LANES), F32)],
        compiler_params=_cparams(("parallel", "arbitrary")),
        name="peer_score",
    )(h_hi, h_lo, wq_hi, wq_lo, subkeys, w_up_bf)


def _peer_select_kernel(s_ref, ia_ref, ib_ref, g_ref, a_ref, *, tn, unroll):
    NK = PEER_KEYS
    sub = lax.broadcasted_iota(jnp.int32, (NK, PEER_SEL), 0)

    def group(gidx, carry):
        pairs = [gidx * unroll + u for u in range(unroll)]
        toks = [2 * p + par for p in pairs for par in range(2)]
        is_a = [sub == ia_ref[pl.ds(t, 1), :] for t in toks]
        onehot_b = [jnp.where(sub == ib_ref[pl.ds(t, 1), :], 1.0, 0.0).astype(BF16) for t in toks]
        vals = []
        for u, p in enumerate(pairs):
            words = s_ref[p]
            for par in range(2):
                k = 2 * u + par
                x = pltpu.unpack_elementwise(words, index=par, packed_dtype=BF16, unpacked_dtype=F32)
                r = jnp.dot(x.astype(BF16), onehot_b[k], preferred_element_type=F32)
                vals.append(jnp.sum(jnp.where(is_a[k], r, 0.0), axis=0, keepdims=True))
        for u, p in enumerate(pairs):
            ys = []
            for par in range(2):
                k = 2 * u + par
                v = vals[k]
                act = g_ref[pl.ds(toks[k], 1), :] * (0.5 * v * (1.0 + lax.erf(v * (2.0 ** -0.5))))
                wa = jnp.where(is_a[k], act, 0.0).astype(BF16)
                ys.append(lax.dot_general(wa, onehot_b[k], (((1,), (1,)), ((), ())), preferred_element_type=F32))
            a_ref[p] = lax.bitcast_convert_type(pltpu.pack_elementwise(ys, packed_dtype=BF16), jnp.uint32)
        return carry

    lax.fori_loop(0, tn // 2 // unroll, group, 0)


def _peer_select(s, ia, ib, gate, tn=128, unroll=8):
    n = ia.shape[0]
    big = pl.BlockSpec((tn // 2, PEER_KEYS, PEER_KEYS), lambda i: (i, 0, 0))
    small = pl.BlockSpec((tn, PEER_SEL), lambda i: (i, 0))
    return pl.pallas_call(
        functools.partial(_peer_select_kernel, tn=tn, unroll=unroll),
        grid=(n // tn,),
        in_specs=[big, small, small, small],
        out_specs=big,
        out_shape=jax.ShapeDtypeStruct(s.shape, jnp.uint32),
        compiler_params=_cparams(("parallel",)),
        name="peer_select",
    )(s, ia, ib, gate)


def _peer_down_kernel(a_ref, w_ref, x_ref, g_ref, o_ref, acc_scr, *, final_norm):
    k = pl.program_id(1)

    @pl.when(k == 0)
    def _():
        acc_scr[...] = jnp.zeros(acc_scr.shape, F32)

    pairs = a_ref.shape[0] // DOWN_CHUNKS
    for r in range(DOWN_CHUNKS):
        lhs = _from_pair_major(a_ref[r * pairs:(r + 1) * pairs])
        acc_scr[2 * r * pairs:2 * (r + 1) * pairs, :] += jnp.dot(lhs, w_ref[...], preferred_element_type=F32)

    @pl.when(k == pl.num_programs(1) - 1)
    def _():
        x = x_ref[...] + acc_scr[...]
        if final_norm:
            x = x * lax.rsqrt(jnp.mean(x * x, axis=-1, keepdims=True) + EPS) * g_ref[...]
        o_ref[...] = x


def _peer_down(a, w_down_bf, x1, norm_g, final_norm, tm=1024):
    n = x1.shape[0]
    e, d = w_down_bf.shape
    tk = PAIR_ROWS * PEER_KEYS
    return pl.pallas_call(
        functools.partial(_peer_down_kernel, final_norm=final_norm),
        grid=(n // tm, e // tk),
        in_specs=[pl.BlockSpec((tm // 2, PAIR_ROWS, PEER_KEYS), lambda i, k: (i, k, 0)),
                  pl.BlockSpec((tk, d), lambda i, k: (k, 0)),
                  pl.BlockSpec((tm, d), lambda i, k: (i, 0)), pl.BlockSpec((1, d), lambda i, k: (0, 0))],
        out_specs=pl.BlockSpec((tm, d), lambda i, k: (i, 0)),
        out_shape=jax.ShapeDtypeStruct((n, d), F32),
        scratch_shapes=[pltpu.VMEM((tm, d), F32)],
        compiler_params=_cparams(("parallel", "arbitrary")),
        name="peer_down",
    )(a, w_down_bf, x1, norm_g.reshape(1, d))


def _peer(x1, h2b, h2lo, w_query, subkeys, w_up, w_down, norm_g, final_norm):
    s, ia, ib, gate = _peer_score(h2b, h2lo, w_query, subkeys, w_up.astype(BF16))
    a = _peer_select(s, ia, ib, gate)
    return _peer_down(a, w_down.astype(BF16), x1, norm_g, final_norm)


def kernel(x_prompt, x_sample, state_delta, state_conv, cache_win0, cache_win1, cache_win2, state_pool, norm_mix, w_in, conv_w, a_log, dt_bias, dn_norm, w_pool, pool_scale, w_branch, w_out, norm_ffn, peer_query, peer_subkeys, peer_up, peer_down, norm_final):
    Bp, Tp, D = x_prompt.shape
    Bs, Ts, _ = x_sample.shape
    depth = w_in.shape[0]
    past = PAST_LEN
    npr = Bp * Tp
    caches = (cache_win0, cache_win1, cache_win2)
    caches_t = [jnp.transpose(c, (0, 1, 3, 4, 5, 2)) for c in caches]
    x = jnp.concatenate([x_prompt.reshape(npr, D), x_sample.reshape(Bs * Ts, D)], axis=0)
    cos_p, sin_p = _rope_tables(jnp.arange(Tp))
    cos_s, sin_s = _rope_tables(past + jnp.arange(Ts))
    nconv = CONV_W - 1
    dn_w = DN_HEADS * DN_DIM

    delta_p, delta_s, conv_p, conv_s, pool_p, pool_s = [], [], [], [], [], []
    win_p = [[] for _ in SWA_GROUPS]
    new_s = [[] for _ in SWA_GROUPS]
    for l in range(depth):
        P = _inproj(x, norm_mix[l], _pack_w_in(w_in[l]))
        Ps = P[npr:].reshape(Bs, Ts, P_COLS)
        cw8 = jnp.pad(conv_w[l], ((0, 8 - CONV_W), (0, 0)))

        def prompt_tail(rows, col, width):
            return jnp.stack([lax.slice(P, ((b + 1) * Tp - rows, col), ((b + 1) * Tp, col + width)) for b in range(Bp)])

        abT_p = jnp.transpose(lax.slice(P, (0, COL_AB), (npr, COL_AB + 8)).reshape(Bp, Tp, 8), (0, 2, 1))
        abT_s = jnp.transpose(Ps[:, :, COL_AB:COL_AB + 8], (0, 2, 1))
        oa_p, dp = _deltanet(P, abT_p, jnp.zeros((Bp, 8, 3 * dn_w), F32), cw8,
                              jnp.zeros((Bp, DN_HEADS, DN_DIM, DN_DIM), F32), a_log[l], dt_bias[l], dn_norm[l],
                              0, Bp, Tp, 4 * DN_CHUNK, DN_CHUNK)
        oa_s, ds = _deltanet(P, abT_s, jnp.pad(state_conv[l], ((0, 0), (8 - nconv, 0), (0, 0))), cw8,
                              state_delta[l], a_log[l], dt_bias[l], dn_norm[l], npr, Bs, Ts, Ts, min(DN_CHUNK, Ts), nseq=8)
        delta_p.append(dp)
        delta_s.append(ds)
        conv_p.append(prompt_tail(nconv, COL_QKV_A, 3 * dn_w))
        conv_s.append(jnp.concatenate([state_conv[l], Ps[:, :, COL_QKV_A:COL_QKV_A + 3 * dn_w]], axis=1)[:, -nconv:])

        attn_p, attn_s = [], []
        for gi, (window, dil) in enumerate(SWA_GROUPS):
            o_p, ml_p, kr_p = _attn_prompt(P, cos_p, sin_p, gi, Bp, Tp)
            o_s, ml_s, kr_s = _attn_sample(P, cos_s, sin_s, caches_t[gi], l, gi, npr, Bs, Ts)
            attn_p += [o_p, ml_p]
            attn_s += [o_s, ml_s]
            vcol = _qkvb_col(gi, 2)
            keep = min(window, Tp)
            kv_p = jnp.stack([kr_p.reshape(Bp, Tp, SWA_OUT)[:, Tp - keep:], prompt_tail(keep, vcol, SWA_OUT)], axis=2)
            win_p[gi].append(kv_p.reshape(Bp, keep, 2, SWA_HEADS, SWA_DIM))
            kv_s = jnp.stack([kr_s.reshape(Bs, Ts, SWA_OUT), Ps[:, :, vcol:vcol + SWA_OUT]], axis=2)
            new_s[gi].append(kv_s.reshape(Bs, Ts, 2, SWA_HEADS, SWA_DIM))

        wp_bf = w_pool[l].astype(BF16)
        oc_p = _pool(P, jnp.zeros((Bp, POOL_BUF + 1, POOL_WIDTH), F32), wp_bf, pool_scale[l], 0, Bp, Tp, 256, 0)
        oc_s = _pool(P, jnp.pad(state_pool[l], ((0, 0), (1, 0), (0, 0))), wp_bf, pool_scale[l], npr, Bs, Ts, Ts, past)
        pool_p.append(prompt_tail(POOL_BUF, COL_U, POOL_WIDTH))
        pool_s.append(jnp.concatenate([state_pool[l], Ps[:, :, COL_U:COL_U + POOL_WIDTH]], axis=1)[:, -POOL_BUF:])

        wb = w_branch[l].astype(BF16)
        x1, h2b, h2lo = _merge(x, P, [oa_p] + attn_p + [oc_p], [oa_s] + attn_s + [oc_s],
                               wb[:dn_w], wb[dn_w:dn_w + SWA_OUT], wb[dn_w + SWA_OUT:], w_out[l].astype(BF16), norm_ffn[l])
        x = _peer(x1, h2b, h2lo, peer_query[l], peer_subkeys[l], peer_up[l], peer_down[l], norm_final, l == depth - 1)

    y_p = x[:npr].reshape(Bp, Tp, D)
    y_s = x[npr:].reshape(Bs, Ts, D)
    st = jnp.stack
    win_s = []
    for gi, (window, dil) in enumerate(SWA_GROUPS):
        kv_all = jnp.concatenate([caches[gi], st(new_s[gi])], axis=2)
        win_s.append(kv_all[:, :, kv_all.shape[2] - min(window, kv_all.shape[2]):])
    return (y_p, y_s, st(delta_p), st(delta_s), st(conv_p), st(conv_s),
            st(win_p[0]), win_s[0], st(win_p[1]), win_s[1], st(win_p[2]), win_s[2], st(pool_p), st(pool_s))
```
